```python
import math
import jax, jax.numpy as jnp
from jax import lax
import numpy as np

D_MODEL = 1024
BATCH = 16
SEQ = 2048
DEPTH = 1

MIX_WIDTH = D_MODEL
GLA_HEADS = 4
GLA_WIDTH = MIX_WIDTH // 2
GLA_DV = GLA_WIDTH // GLA_HEADS
GLA_DK = GLA_DV // 2
GLA_RANK = 16
GLA_TAU = 16.0
HG_HEADS = 4
HG_WIDTH = MIX_WIDTH - GLA_WIDTH
HG_DK = HG_WIDTH // HG_HEADS
HG_DV = HG_DK
CHUNK = 64
IN_SIZES = (GLA_HEADS * GLA_DK, GLA_HEADS * GLA_DK, GLA_WIDTH, GLA_WIDTH, GLA_RANK,
            HG_WIDTH, HG_WIDTH, HG_WIDTH, HG_WIDTH)
IN_COLS = GLA_HEADS * GLA_DK * 2 + GLA_WIDTH * 2 + GLA_RANK + HG_WIDTH * 4
N_GROUPS = 4
EXPERTS_PER_GROUP = 8
N_EXPERTS = N_GROUPS * EXPERTS_PER_GROUP
EXPERT_FF = D_MODEL // 4
TOP_K = 2
EPS = 1e-6

kernel_name = "hymba_gla_hgrn2_hmoe_adaln"


def rmsnorm(x, g):
    xf = x.astype(jnp.float32)
    r = lax.rsqrt(jnp.mean(xf * xf, axis=-1, keepdims=True) + EPS)
    return xf * r * g.astype(jnp.float32)


def modulate(hf, shift, scale, dtype):
    return (hf * (1.0 + scale[:, None, :].astype(jnp.float32)) + shift[:, None, :].astype(jnp.float32)).astype(dtype)


def chunk_gated_linear_attention(q, k, v, log_g):
    B, S, H, dk = q.shape
    dv = v.shape[-1]
    nc = S // CHUNK

    def to_chunks(t):
        return t.astype(jnp.float32).reshape(B, nc, CHUNK, H, t.shape[-1]).transpose(1, 0, 3, 2, 4)

    causal = jnp.tril(jnp.ones((CHUNK, CHUNK), dtype=bool))

    def step(state, inp):
        qc, kc, vc, gc = inp
        b = jnp.cumsum(gc, axis=2)
        diff = jnp.where(causal[:, :, None], b[:, :, :, None, :] - b[:, :, None, :, :], -jnp.inf)
        attn = jnp.sum(qc[:, :, :, None, :] * kc[:, :, None, :, :] * jnp.exp(diff), axis=-1)
        o = jnp.einsum('bhij,bhjv->bhiv', attn, vc) + jnp.einsum('bhik,bhkv->bhiv', qc * jnp.exp(b), state)
        b_last = b[:, :, -1:, :]
        state = jnp.exp(b_last[:, :, 0, :])[..., None] * state + jnp.einsum(
            'bhjk,bhjv->bhkv', kc * jnp.exp(b_last - b), vc)
        return state, o

    s0 = jnp.zeros((B, H, dk, dv), jnp.float32)
    _, o = lax.scan(step, s0, (to_chunks(q), to_chunks(k), to_chunks(v), to_chunks(log_g)))
    return o.transpose(1, 0, 3, 2, 4).reshape(B, S, H, dv)


def head_rmsnorm(o, g):
    r = lax.rsqrt(jnp.mean(o * o, axis=-1, keepdims=True) + EPS)
    return o * r * g.astype(jnp.float32)


def hybrid_mixer(h, w_in, gla_w_a2, gla_b_a, gla_norm_g, hg_norm_g, lb, w_out):
    B, S, _ = h.shape
    proj = h @ w_in
    offs = []
    acc = 0
    for s in IN_SIZES[:-1]:
        acc += s
        offs.append(acc)
    gq, gk, gv, gg, ga, hq, hf, hi, hg = jnp.split(proj, offs, axis=-1)

    q = gq.reshape(B, S, GLA_HEADS, GLA_DK).astype(jnp.float32) * (GLA_DK ** -0.5)
    k = gk.reshape(B, S, GLA_HEADS, GLA_DK)
    v = gv.reshape(B, S, GLA_HEADS, GLA_DV)
    a_logit = (ga @ gla_w_a2 + gla_b_a).astype(jnp.float32)
    log_alpha = (jax.nn.log_sigmoid(a_logit) / GLA_TAU).reshape(B, S, GLA_HEADS, GLA_DK)
    o_a = chunk_gated_linear_attention(q, k, v, log_alpha)
    o_a = head_rmsnorm(o_a, gla_norm_g) * jax.nn.silu(gg.astype(jnp.float32)).reshape(B, S, GLA_HEADS, GLA_DV)
    o_a = o_a.reshape(B, S, GLA_WIDTH)

    f = lb + (1.0 - lb) * jax.nn.sigmoid(hf.astype(jnp.float32))
    k_h = (1.0 - f).reshape(B, S, HG_HEADS, HG_DK)
    log_f = jnp.log(f).reshape(B, S, HG_HEADS, HG_DK)
    q_h = hq.reshape(B, S, HG_HEADS, HG_DK).astype(jnp.float32) * (HG_DK ** -0.5)
    i_h = hi.reshape(B, S, HG_HEADS, HG_DV)
    o_b = chunk_gated_linear_attention(q_h, k_h, i_h, log_f)
    o_b = head_rmsnorm(o_b, hg_norm_g) * jax.nn.silu(hg.astype(jnp.float32)).reshape(B, S, HG_HEADS, HG_DV)
    o_b = o_b.reshape(B, S, HG_WIDTH)

    o = jnp.concatenate([o_a, o_b], axis=-1).astype(h.dtype)
    return o @ w_out


def hierarchical_moe(h, w_rg, b_rg, w_re, b_re, w_gate, w_up, w_down):
    B, S, D = h.shape
    hf = h.reshape(B * S, D)
    g_probs = jax.nn.softmax((hf @ w_rg + b_rg).astype(jnp.float32), axis=-1)
    g_p, g_idx = lax.top_k(g_probs, 1)
    e_logits = (hf @ w_re + b_re).astype(jnp.float32).reshape(B * S, N_GROUPS, EXPERTS_PER_GROUP)
    sel = jnp.take_along_axis(e_logits, g_idx[:, :, None], axis=1)[:, 0]
    e_probs = jax.nn.softmax(sel, axis=-1)
    e_p, e_idx = lax.top_k(e_probs, TOP_K)
    e_p = e_p / jnp.sum(e_p, axis=-1, keepdims=True)
    weights = g_p * e_p
    expert_ids = g_idx * EXPERTS_PER_GROUP + e_idx
    combine = jnp.sum(jax.nn.one_hot(expert_ids, N_EXPERTS, dtype=jnp.float32) * weights[..., None], axis=1)
    combine = combine.reshape(B, S, N_EXPERTS).astype(h.dtype)

    def per_sequence(args):
        xs, cw = args
        a = jax.nn.silu(jnp.einsum('sd,edf->sef', xs, w_gate)) * jnp.einsum('sd,edf->sef', xs, w_up)
        return jnp.einsum('sef,efd->sd', a * cw[..., None], w_down)

    return lax.map(per_sequence, (h, combine))


def setup_inputs(seed: int = 0) -> dict:
    key = jax.random.key(seed)
    ks = jax.random.split(key, 24)
    f32 = jnp.float32

    def nrm(k, shape, fan_in, mult=1.0):
        return jax.random.normal(k, shape, f32) * (mult * fan_in ** -0.5)

    L, D = DEPTH, D_MODEL
    return {
        "x": jax.random.normal(ks[0], (BATCH, SEQ, D), f32),
        "c": jax.random.normal(ks[1], (BATCH, D), f32),
        "w_ada": nrm(ks[2], (L, D, 6 * D), D, 0.5),
        "b_ada": 0.02 * jax.random.normal(ks[3], (L, 6 * D), f32),
        "norm1_g": 1.0 + 0.02 * jax.random.normal(ks[4], (L, D), f32),
        "w_in": nrm(ks[5], (L, D, IN_COLS), D),
        "gla_w_a2": nrm(ks[6], (L, GLA_RANK, GLA_HEADS * GLA_DK), GLA_RANK),
        "gla_b_a": 0.1 * jax.random.normal(ks[7], (L, GLA_HEADS * GLA_DK), f32),
        "gla_norm_g": 1.0 + 0.02 * jax.random.normal(ks[8], (L, GLA_DV), f32),
        "hg_norm_g": 1.0 + 0.02 * jax.random.normal(ks[9], (L, HG_DV), f32),
        "hg_lb_logits": 0.1 * jax.random.normal(ks[10], (L + 1, HG_WIDTH), f32),
        "w_out": nrm(ks[11], (L, MIX_WIDTH, D), MIX_WIDTH),
        "norm2_g": 1.0 + 0.02 * jax.random.normal(ks[12], (L, D), f32),
        "w_rg": nrm(ks[13], (L, D, N_GROUPS), D),
        "b_rg": 0.01 * jax.random.normal(ks[14], (L, N_GROUPS), f32),
        "w_re": nrm(ks[15], (L, D, N_EXPERTS), D),
        "b_re": 0.01 * jax.random.normal(ks[16], (L, N_EXPERTS), f32),
        "w_exp_gate": nrm(ks[17], (L, N_EXPERTS, D, EXPERT_FF), D),
        "w_exp_up": nrm(ks[18], (L, N_EXPERTS, D, EXPERT_FF), D),
        "w_exp_down": nrm(ks[19], (L, N_EXPERTS, EXPERT_FF, D), EXPERT_FF),
        "w_ada_final": nrm(ks[20], (D, 2 * D), D, 0.5),
        "b_ada_final": 0.02 * jax.random.normal(ks[21], (2 * D,), f32),
        "norm_f_g": 1.0 + 0.02 * jax.random.normal(ks[22], (D,), f32),
    }


def reference(x, c, w_ada, b_ada, norm1_g, w_in, gla_w_a2, gla_b_a, gla_norm_g, hg_norm_g,
              hg_lb_logits, w_out, norm2_g, w_rg, b_rg, w_re, b_re, w_exp_gate, w_exp_up,
              w_exp_down, w_ada_final, b_ada_final, norm_f_g):
    dt = x.dtype
    c_act = jax.nn.silu(c)
    lb_all = jnp.cumsum(jax.nn.softmax(hg_lb_logits.astype(jnp.float32), axis=0), axis=0)
    for l in range(DEPTH):
        mod = c_act @ w_ada[l] + b_ada[l]
        sh1, sc1, gt1, sh2, sc2, gt2 = jnp.split(mod, 6, axis=-1)
        h = modulate(rmsnorm(x, norm1_g[l]), sh1, sc1, dt)
        mix = hybrid_mixer(h, w_in[l], gla_w_a2[l], gla_b_a[l], gla_norm_g[l], hg_norm_g[l],
                           lb_all[l], w_out[l])
        x = x + gt1[:, None, :] * mix
        h2 = modulate(rmsnorm(x, norm2_g[l]), sh2, sc2, dt)
        ffn = hierarchical_moe(h2, w_rg[l], b_rg[l], w_re[l], b_re[l],
                               w_exp_gate[l], w_exp_up[l], w_exp_down[l])
        x = x + gt2[:, None, :] * ffn
    mod_f = c_act @ w_ada_final + b_ada_final
    sh_f, sc_f = jnp.split(mod_f, 2, axis=-1)
    return modulate(rmsnorm(x, norm_f_g), sh_f, sc_f, dt)
```

```python
import functools

import jax
import jax.numpy as jnp
from jax import lax
from jax.experimental import pallas as pl
from jax.experimental.pallas import tpu as pltpu

F32 = jnp.float32
BF16 = jnp.bfloat16

D_MODEL = 1024
GLA_HEADS, GLA_DK, GLA_DV, GLA_RANK, GLA_TAU = 4, 64, 128, 16, 16.0
HG_HEADS, HG_DK, HG_DV = 4, 128, 128
GLA_QK = GLA_HEADS * GLA_DK
GLA_W = GLA_HEADS * GLA_DV
HG_W = HG_HEADS * HG_DK
CHUNK = 64
N_GROUPS, EXPERTS_PER_GROUP, N_EXPERTS, EXPERT_FF = 4, 8, 32, 256
EPS = 1e-6
LANES = 128

C_GQ, C_GK, C_GV, C_GG, C_GA = 0, 256, 512, 1024, 1536
C_HQ, C_HF, C_HI, C_HG, C_END = 1664, 2176, 2688, 3200, 3712

MIX_TILE = 256
ROW_TILE = 256
EXP_CLAMP = 80.0
VMEM_LIMIT = 56 * 1024 * 1024


def _silu(t):
    return t * jax.nn.sigmoid(t)


def _dot(a, b):
    return jnp.dot(a, b, preferred_element_type=F32)


def _dot_nt(a, b):
    return lax.dot_general(a, b, (((1,), (1,)), ((), ())), preferred_element_type=F32)


def _dot_tn(a, b):
    return lax.dot_general(a, b, (((0,), (0,)), ((), ())), preferred_element_type=F32)


def _ada_kernel(c_ref, w_ref, b_ref, o_ref):
    ca = _silu(c_ref[...]).astype(BF16)
    o_ref[...] = _dot(ca, w_ref[...].astype(BF16)) + b_ref[...]


def _ada(c, w, b, tn):
    bsz, dm = c.shape
    n = w.shape[1]
    return pl.pallas_call(
        _ada_kernel,
        grid=(n // tn,),
        in_specs=[pl.BlockSpec((bsz, dm), lambda j: (0, 0)),
                  pl.BlockSpec((dm, tn), lambda j: (0, j)),
                  pl.BlockSpec((1, tn), lambda j: (0, j))],
        out_specs=pl.BlockSpec((bsz, tn), lambda j: (0, j)),
        out_shape=jax.ShapeDtypeStruct((bsz, n), F32),
        compiler_params=pltpu.CompilerParams(vmem_limit_bytes=VMEM_LIMIT),
        name="ada",
    )(c, w, b.reshape(1, n))


def _split2(t):
    hi = t.astype(BF16)
    lo = (t - hi.astype(F32)).astype(BF16)
    return hi, lo


def _chunk_mix(q, k, v, ld, st_ref, tril_b, causal, heads, dk, dv):
    c = q.shape[0]
    ld_hi, ld_lo = _split2(ld)
    b = _dot(tril_b, ld_hi) + _dot(tril_b, ld_lo)
    bm = b[c // 2 - 1:c // 2, :]
    bl = b[c - 1:c, :]
    eq = jnp.exp(jnp.minimum(b - bm, EXP_CLAMP))
    ek = jnp.exp(jnp.minimum(bm - b, EXP_CLAMP))
    qt = q * eq
    kt = k * ek
    qs = (qt * jnp.exp(bm)).astype(BF16)
    ks = (kt * jnp.exp(bl - bm)).astype(BF16)
    qt = qt.astype(BF16)
    kt = kt.astype(BF16)
    dec = jnp.exp(bl)
    vb = v.astype(BF16)
    outs = []
    for h in range(heads):
        ks_ = slice(h * dk, (h + 1) * dk)
        vs_ = slice(h * dv, (h + 1) * dv)
        attn = _dot_nt(qt[:, ks_], kt[:, ks_])
        attn = jnp.where(causal, attn, 0.0).astype(BF16)
        st = st_ref[h]
        o = _dot(attn, vb[:, vs_]) + _dot_nt(qs[:, ks_], st.astype(BF16))
        st_ref[h] = st * dec[:, ks_] + _dot_tn(vb[:, vs_], ks[:, ks_])
        outs.append(o)
    return jnp.concatenate(outs, axis=-1)


def _head_norm(o, g, heads, dv):
    outs = []
    for h in range(heads):
        oh = o[:, h * dv:(h + 1) * dv]
        r = lax.rsqrt(jnp.mean(oh * oh, axis=-1, keepdims=True) + EPS)
        outs.append(oh * r * g)
    return jnp.concatenate(outs, axis=-1)


def _mixer_kernel(x_ref, mod_ref, n1g_ref, win_ref, wa2_ref, ba_ref, glag_ref, hgg_ref, lbl_ref,
                  wout_ref, n2g_ref, wr_ref, br_ref,
                  x1_ref, h2_ref, meta_ref, cnt_ref,
                  proj_ref, ldg_ref, o_ref, sg_ref, sh_ref, carry_ref):
    tile = x_ref.shape[1]
    bi = pl.program_id(0)
    si = pl.program_id(1)

    @pl.when(si == 0)
    def _():
        sg_ref[...] = jnp.zeros_like(sg_ref)
        sh_ref[...] = jnp.zeros_like(sh_ref)

    @pl.when((bi == 0) & (si == 0))
    def _():
        carry_ref[...] = jnp.zeros_like(carry_ref)

    mod = mod_ref[0]
    sh1, sc1, gt1 = mod[0:1], mod[1:2], mod[2:3]
    sh2, sc2 = mod[3:4], mod[4:5]

    x = x_ref[0]
    r1 = lax.rsqrt(jnp.mean(x * x, axis=-1, keepdims=True) + EPS)
    h = (x * r1 * n1g_ref[...]) * (1.0 + sc1) + sh1
    proj_ref[...] = _dot(h.astype(BF16), win_ref[...])

    a_logit = _dot(proj_ref[:, C_GA:C_GA + LANES].astype(BF16), wa2_ref[...]) + ba_ref[...]
    log_sig = jnp.minimum(a_logit, 0.0) - jnp.log(1.0 + jnp.exp(-jnp.abs(a_logit)))
    ldg_ref[...] = log_sig * (1.0 / GLA_TAU)

    lbl = lbl_ref[...]
    lmax = jnp.max(lbl, axis=0, keepdims=True)
    lexp = jnp.exp(lbl - lmax)
    lb = lexp[0:1] / jnp.sum(lexp, axis=0, keepdims=True)

    row = lax.broadcasted_iota(jnp.int32, (CHUNK, CHUNK), 0)
    col = lax.broadcasted_iota(jnp.int32, (CHUNK, CHUNK), 1)
    causal = row >= col
    tril_b = causal.astype(BF16)
    glag = glag_ref[...]
    hgg = hgg_ref[...]

    def chunk_body(ci, carry):
        rows = pl.ds(pl.multiple_of(ci * CHUNK, CHUNK), CHUNK)
        q = proj_ref[rows, C_GQ:C_GK] * (GLA_DK ** -0.5)
        k = proj_ref[rows, C_GK:C_GV]
        v = proj_ref[rows, C_GV:C_GG]
        gg = proj_ref[rows, C_GG:C_GA]
        o_a = _chunk_mix(q, k, v, ldg_ref[rows, :], sg_ref, tril_b, causal, GLA_HEADS, GLA_DK, GLA_DV)
        o_a = _head_norm(o_a, glag, GLA_HEADS, GLA_DV) * _silu(gg)
        o_ref[rows, 0:GLA_W] = o_a.astype(BF16)
        hq = proj_ref[rows, C_HQ:C_HF] * (HG_DK ** -0.5)
        sig = jax.nn.sigmoid(proj_ref[rows, C_HF:C_HI])
        f = lb + (1.0 - lb) * sig
        kh = (1.0 - lb) * (1.0 - sig)
        hi = proj_ref[rows, C_HI:C_HG]
        hg = proj_ref[rows, C_HG:C_END]
        o_b = _chunk_mix(hq, kh, hi, jnp.log(f), sh_ref, tril_b, causal, HG_HEADS, HG_DK, HG_DV)
        o_b = _head_norm(o_b, hgg, HG_HEADS, HG_DV) * _silu(hg)
        o_ref[rows, GLA_W:GLA_W + HG_W] = o_b.astype(BF16)
        return carry

    lax.fori_loop(0, tile // CHUNK, chunk_body, 0)

    mix = _dot(o_ref[...], wout_ref[...])
    x1 = x_ref[0] + gt1 * mix
    x1_ref[0] = x1
    r2 = lax.rsqrt(jnp.mean(x1 * x1, axis=-1, keepdims=True) + EPS)
    h2 = (x1 * r2 * n2g_ref[...]) * (1.0 + sc2) + sh2
    h2_ref[0] = h2

    logits = _dot(h2.astype(BF16), wr_ref[...]) + br_ref[...]
    lane = lax.broadcasted_iota(jnp.int32, (tile, LANES), 1)
    neg = jnp.float32(-jnp.inf)
    big = jnp.int32(1 << 20)
    gl = jnp.where((lane >= N_EXPERTS) & (lane < N_EXPERTS + N_GROUPS), logits, neg)
    gmax = jnp.max(gl, axis=-1, keepdims=True)
    gidx = jnp.min(jnp.where(gl == gmax, lane, big), axis=-1, keepdims=True) - N_EXPERTS
    g_p = 1.0 / jnp.sum(jnp.exp(gl - gmax), axis=-1, keepdims=True)
    in_group = (lane >= gidx * EXPERTS_PER_GROUP) & (lane < (gidx + 1) * EXPERTS_PER_GROUP)
    el = jnp.where(in_group, logits, neg)
    m1 = jnp.max(el, axis=-1, keepdims=True)
    i1 = jnp.min(jnp.where(el == m1, lane, big), axis=-1, keepdims=True)
    el2 = jnp.where(lane == i1, neg, el)
    m2 = jnp.max(el2, axis=-1, keepdims=True)
    i2 = jnp.min(jnp.where(el2 == m2, lane, big), axis=-1, keepdims=True)
    t = jnp.exp(m2 - m1)
    w1 = g_p / (1.0 + t)
    w2 = g_p * t / (1.0 + t)

    sel1 = lane == i1
    sel2 = lane == i2
    onehot = (sel1 | sel2).astype(BF16)
    trow = lax.broadcasted_iota(jnp.int32, (tile, tile), 0)
    tcol = lax.broadcasted_iota(jnp.int32, (tile, tile), 1)
    strict = (trow > tcol).astype(BF16)
    cum = _dot(strict, onehot) + carry_ref[...]
    rank1 = jnp.sum(jnp.where(sel1, cum, 0.0), axis=-1, keepdims=True)
    rank2 = jnp.sum(jnp.where(sel2, cum, 0.0), axis=-1, keepdims=True)
    carry_ref[...] = carry_ref[...] + jnp.sum(onehot.astype(F32), axis=0, keepdims=True)
    cnt_ref[...] = carry_ref[...]

    meta = jnp.where(lane == 0, i1.astype(F32), 0.0)
    meta = jnp.where(lane == 1, i2.astype(F32), meta)
    meta = jnp.where(lane == 2, w1, meta)
    meta = jnp.where(lane == 3, w2, meta)
    meta = jnp.where(lane == 4, rank1, meta)
    meta = jnp.where(lane == 5, rank2, meta)
    meta_ref[0] = meta


def _mixer(x, modall, n1g, win, wa2, ba, glag, hgg, lbl, wout, n2g, wr, br):
    bsz, seq, dm = x.shape
    tile = min(MIX_TILE, seq)
    grid = (bsz, seq // tile)
    const = lambda shape: pl.BlockSpec(shape, lambda b, s: (0,) * len(shape))
    tok = lambda width: pl.BlockSpec((1, tile, width), lambda b, s: (b, s, 0))
    return pl.pallas_call(
        _mixer_kernel,
        grid=grid,
        in_specs=[tok(dm),
                  pl.BlockSpec((1, 8, dm), lambda b, s: (b, 0, 0)),
                  const((1, dm)), const((dm, C_END)), const((LANES, GLA_QK)), const((1, GLA_QK)),
                  const((1, GLA_DV)), const((1, HG_DV)), const((2, HG_W)),
                  const((dm, dm)), const((1, dm)), const((dm, LANES)), const((1, LANES))],
        out_specs=[tok(dm), tok(dm), tok(LANES), const((1, LANES))],
        out_shape=[jax.ShapeDtypeStruct((bsz, seq, dm), F32),
                   jax.ShapeDtypeStruct((bsz, seq, dm), F32),
                   jax.ShapeDtypeStruct((bsz, seq, LANES), F32),
                   jax.ShapeDtypeStruct((1, LANES), F32)],
        scratch_shapes=[pltpu.VMEM((tile, C_END), F32),
                        pltpu.VMEM((tile, GLA_QK), F32),
                        pltpu.VMEM((tile, dm), BF16),
                        pltpu.VMEM((GLA_HEADS, GLA_DV, GLA_DK), F32),
                        pltpu.VMEM((HG_HEADS, HG_DV, HG_DK), F32),
                        pltpu.VMEM((1, LANES), F32)],
        compiler_params=pltpu.CompilerParams(
            dimension_semantics=("arbitrary", "arbitrary"), vmem_limit_bytes=VMEM_LIMIT),
        name="mixer",
    )(x, modall, n1g, win, wa2, ba, glag, hgg, lbl, wout, n2g, wr, br)


def _dispatch_kernel(slot_ref, h2_ref, xs_in_ref, xs_ref, sem):
    del xs_in_ref
    tile = h2_ref.shape[0]

    def copy(t, j):
        return pltpu.make_async_copy(h2_ref.at[pl.ds(t, 1), :],
                                     xs_ref.at[pl.ds(slot_ref[2 * t + j], 1), :], sem)

    def issue(t, carry):
        copy(t, 0).start()
        copy(t, 1).start()
        return carry

    def drain(t, carry):
        copy(t, 0).wait()
        copy(t, 1).wait()
        return carry

    lax.fori_loop(0, tile, issue, 0)
    lax.fori_loop(0, tile, drain, 0)


def _dispatch(slots, h2, xs_init):
    n, dm = h2.shape
    tile = min(ROW_TILE, n)
    return pl.pallas_call(
        _dispatch_kernel,
        grid=(n // tile,),
        in_specs=[pl.BlockSpec((2 * tile,), lambda i: (i,), memory_space=pltpu.SMEM),
                  pl.BlockSpec((tile, dm), lambda i: (i, 0)),
                  pl.BlockSpec(memory_space=pl.ANY)],
        out_specs=pl.BlockSpec(memory_space=pl.ANY),
        out_shape=jax.ShapeDtypeStruct(xs_init.shape, xs_init.dtype),
        scratch_shapes=[pltpu.SemaphoreType.DMA(())],
        input_output_aliases={2: 0},
        compiler_params=pltpu.CompilerParams(
            dimension_semantics=("arbitrary",), vmem_limit_bytes=VMEM_LIMIT),
        name="dispatch",
    )(slots, h2, xs_init)


def _experts_kernel(te_ref, nu_ref, xs_ref, wg_ref, wu_ref, wd_ref, ys_ref):
    i = pl.program_id(0)

    @pl.when(i < nu_ref[0])
    def _():
        xb = xs_ref[...].astype(BF16)
        g = _dot(xb, wg_ref[0].astype(BF16))
        u = _dot(xb, wu_ref[0].astype(BF16))
        a = (_silu(g) * u).astype(BF16)
        ys_ref[...] = _dot(a, wd_ref[0].astype(BF16))


def _experts(tile_expert, n_used, xs, wg, wu, wd, n_tiles):
    rows, dm = xs.shape
    tm = ROW_TILE
    ff = wg.shape[-1]

    def row_map(i, te, nu):
        return (jnp.minimum(i, nu[0] - 1), 0)

    def out_map(i, te, nu):
        return (jnp.where(i < nu[0], i, n_tiles), 0)

    return pl.pallas_call(
        _experts_kernel,
        grid_spec=pltpu.PrefetchScalarGridSpec(
            num_scalar_prefetch=2,
            grid=(n_tiles,),
            in_specs=[pl.BlockSpec((tm, dm), row_map),
                      pl.BlockSpec((1, dm, ff), lambda i, te, nu: (te[i], 0, 0)),
                      pl.BlockSpec((1, dm, ff), lambda i, te, nu: (te[i], 0, 0)),
                      pl.BlockSpec((1, ff, dm), lambda i, te, nu: (te[i], 0, 0))],
            out_specs=pl.BlockSpec((tm, dm), out_map)),
        out_shape=jax.ShapeDtypeStruct(((n_tiles + 1) * tm, dm), F32),
        compiler_params=pltpu.CompilerParams(
            dimension_semantics=("arbitrary",), vmem_limit_bytes=VMEM_LIMIT),
        name="experts",
    )(tile_expert, n_used, xs, wg, wu, wd)


def _combine_kernel(slot_ref, x1_ref, meta_ref, mod_ref, nfg_ref, ys_ref, out_ref, buf_ref, sem):
    tile = x1_ref.shape[1]

    def copy(t, j):
        return pltpu.make_async_copy(ys_ref.at[pl.ds(slot_ref[2 * t + j], 1), :],
                                     buf_ref.at[j, pl.ds(t, 1), :], sem)

    def issue(t, carry):
        copy(t, 0).start()
        copy(t, 1).start()
        return carry

    def drain(t, carry):
        copy(t, 0).wait()
        copy(t, 1).wait()
        return carry

    lax.fori_loop(0, tile, issue, 0)
    lax.fori_loop(0, tile, drain, 0)

    mod = mod_ref[0]
    gt2, shf, scf = mod[5:6], mod[6:7], mod[7:8]
    meta = meta_ref[0]
    w1 = meta[:, 2:3]
    w2 = meta[:, 3:4]
    ffn = w1 * buf_ref[0] + w2 * buf_ref[1]
    x2 = x1_ref[0] + gt2 * ffn
    r = lax.rsqrt(jnp.mean(x2 * x2, axis=-1, keepdims=True) + EPS)
    out_ref[0] = (x2 * r * nfg_ref[...]) * (1.0 + scf) + shf


def _combine(slots, x1, meta, modall, nfg, ys):
    bsz, seq, dm = x1.shape
    tile = min(ROW_TILE, seq)
    spt = seq // tile
    return pl.pallas_call(
        _combine_kernel,
        grid=(bsz, spt),
        in_specs=[pl.BlockSpec((2 * tile,), lambda b, s: (b * spt + s,), memory_space=pltpu.SMEM),
                  pl.BlockSpec((1, tile, dm), lambda b, s: (b, s, 0)),
                  pl.BlockSpec((1, tile, LANES), lambda b, s: (b, s, 0)),
                  pl.BlockSpec((1, 8, dm), lambda b, s: (b, 0, 0)),
                  pl.BlockSpec((1, dm), lambda b, s: (0, 0)),
                  pl.BlockSpec(memory_space=pl.ANY)],
        out_specs=pl.BlockSpec((1, tile, dm), lambda b, s: (b, s, 0)),
        out_shape=jax.ShapeDtypeStruct((bsz, seq, dm), F32),
        scratch_shapes=[pltpu.VMEM((2, tile, dm), F32), pltpu.SemaphoreType.DMA(())],
        compiler_params=pltpu.CompilerParams(
            dimension_semantics=("arbitrary", "arbitrary"), vmem_limit_bytes=VMEM_LIMIT),
        name="combine",
    )(slots, x1, meta, modall, nfg, ys)


def kernel(x, c, w_ada, b_ada, norm1_g, w_in, gla_w_a2, gla_b_a, gla_norm_g, hg_norm_g, hg_lb_logits,
           w_out, norm2_g, w_rg, b_rg, w_re, b_re, w_exp_gate, w_exp_up, w_exp_down, w_ada_final,
           b_ada_final, norm_f_g):
    bsz, seq, dm = x.shape
    n_tok = bsz * seq

    mod = _ada(c, w_ada[0], b_ada[0], 1536)
    mod_f = _ada(c, w_ada_final, b_ada_final, 1024)
    modall = jnp.concatenate([mod.reshape(bsz, 6, dm), mod_f.reshape(bsz, 2, dm)], axis=1)

    wi = w_in[0]
    ga0 = 2 * GLA_QK + 2 * GLA_W
    win = jnp.concatenate([wi[:, :ga0],
                           jnp.pad(wi[:, ga0:ga0 + GLA_RANK], ((0, 0), (0, LANES - GLA_RANK))),
                           wi[:, ga0 + GLA_RANK:]], axis=1).astype(BF16)
    wa2 = jnp.pad(gla_w_a2[0], ((0, LANES - GLA_RANK), (0, 0))).astype(BF16)
    pad_r = LANES - N_EXPERTS - N_GROUPS
    wr = jnp.pad(jnp.concatenate([w_re[0], w_rg[0]], axis=1), ((0, 0), (0, pad_r))).astype(BF16)
    br = jnp.pad(jnp.concatenate([b_re[0], b_rg[0]]), (0, pad_r)).reshape(1, LANES)

    x1, h2, meta, cnt = _mixer(
        x, modall, norm1_g[0].reshape(1, dm), win, wa2, gla_b_a[0].reshape(1, GLA_QK),
        gla_norm_g[0].reshape(1, GLA_DV), hg_norm_g[0].reshape(1, HG_DV), hg_lb_logits,
        w_out[0].astype(BF16), norm2_g[0].reshape(1, dm), wr, br)

    tm = ROW_TILE
    n_tiles = (2 * n_tok) // tm + N_EXPERTS
    counts = cnt[0, :N_EXPERTS].astype(jnp.int32)
    padded = ((counts + tm - 1) // tm) * tm
    ends = jnp.cumsum(padded)
    offs = ends - padded
    meta2 = meta.reshape(n_tok, LANES)
    ids = meta2[:, 0:2].astype(jnp.int32)
    ranks = meta2[:, 4:6].astype(jnp.int32)
    slots = (offs[ids] + ranks).reshape(2 * n_tok)
    n_used = (ends[-1] // tm).astype(jnp.int32).reshape(1)
    tile_start = jnp.arange(n_tiles, dtype=jnp.int32) * tm
    tile_start = jnp.minimum(tile_start, ends[-1] - tm)
    tile_expert = jnp.searchsorted(ends, tile_start, side="right").astype(jnp.int32)

    xs = _dispatch(slots, h2.reshape(n_tok, dm), jnp.zeros((n_tiles * tm, dm), F32))
    ys = _experts(tile_expert, n_used, xs, w_exp_gate[0], w_exp_up[0], w_exp_down[0], n_tiles)
    return _combine(slots, x1, meta, modall, norm_f_g.reshape(1, dm), ys)
```

```python
import functools

import jax
import jax.numpy as jnp
from jax import lax
from jax.experimental import pallas as pl
from jax.experimental.pallas import tpu as pltpu

F32 = jnp.float32
BF16 = jnp.bfloat16

D_MODEL = 1024
GLA_HEADS, GLA_DK, GLA_DV, GLA_RANK, GLA_TAU = 4, 64, 128, 16, 16.0
HG_HEADS, HG_DK, HG_DV = 4, 128, 128
GLA_QK = GLA_HEADS * GLA_DK
GLA_W = GLA_HEADS * GLA_DV
HG_W = HG_HEADS * HG_DK
CHUNK = 64
N_GROUPS, EXPERTS_PER_GROUP, N_EXPERTS, EXPERT_FF = 4, 8, 32, 256
EPS = 1e-6
LANES = 128

C_GQ, C_GK, C_GV, C_GG, C_GA = 0, 256, 512, 1024, 1536
C_HQ, C_HF, C_HI, C_HG, C_END = 1664, 2176, 2688, 3200, 3712

MIX_TILE = 256
TOK_TILE = 256
EXP_TILE = 512
EXP_CLAMP = 80.0
VMEM_LIMIT = 56 * 1024 * 1024


def _silu(t):
    return t * jax.nn.sigmoid(t)


def _dot(a, b):
    return jnp.dot(a, b, preferred_element_type=F32)


def _dot_nt(a, b):
    return lax.dot_general(a, b, (((1,), (1,)), ((), ())), preferred_element_type=F32)


def _dot_tn(a, b):
    return lax.dot_general(a, b, (((0,), (0,)), ((), ())), preferred_element_type=F32)


def _ada_kernel(c_ref, w_ref, b_ref, o_ref):
    ca = _silu(c_ref[...]).astype(BF16)
    o_ref[...] = _dot(ca, w_ref[...].astype(BF16)) + b_ref[...]


def _ada(c, w, b, tn):
    bsz, dm = c.shape
    n = w.shape[1]
    return pl.pallas_call(
        _ada_kernel,
        grid=(n // tn,),
        in_specs=[pl.BlockSpec((bsz, dm), lambda j: (0, 0)),
                  pl.BlockSpec((dm, tn), lambda j: (0, j)),
                  pl.BlockSpec((1, tn), lambda j: (0, j))],
        out_specs=pl.BlockSpec((bsz, tn), lambda j: (0, j)),
        out_shape=jax.ShapeDtypeStruct((bsz, n), F32),
        compiler_params=pltpu.CompilerParams(vmem_limit_bytes=VMEM_LIMIT),
        name="ada",
    )(c, w, b.reshape(1, n))


def _split2(t):
    hi = t.astype(BF16)
    lo = (t - hi.astype(F32)).astype(BF16)
    return hi, lo


def _recurrence(q, k, v, ld, st_ref, blocktril, causal, heads, dk, dv):
    t, w = q.shape
    nc = t // CHUNK
    ld_hi, ld_lo = _split2(ld)
    b = (_dot(blocktril, ld_hi) + _dot(blocktril, ld_lo)).reshape(nc, CHUNK, w)
    bm = b[:, CHUNK // 2 - 1:CHUNK // 2, :]
    bl = b[:, CHUNK - 1:CHUNK, :]
    qt = q.reshape(nc, CHUNK, w) * jnp.exp(jnp.minimum(b - bm, EXP_CLAMP))
    kt = k.reshape(nc, CHUNK, w) * jnp.exp(jnp.minimum(bm - b, EXP_CLAMP))
    qs = (qt * jnp.exp(bm)).astype(BF16).reshape(t, w)
    ks = (kt * jnp.exp(bl - bm)).astype(BF16).reshape(t, w)
    qt = qt.astype(BF16).reshape(t, w)
    kt = kt.astype(BF16).reshape(t, w)
    dec = jnp.exp(bl)
    vb = v.astype(BF16)
    states = [st_ref[h] for h in range(heads)]
    out_rows = []
    for c in range(nc):
        rs = slice(c * CHUNK, (c + 1) * CHUNK)
        outs = []
        for h in range(heads):
            ks_ = slice(h * dk, (h + 1) * dk)
            vs_ = slice(h * dv, (h + 1) * dv)
            attn = _dot_nt(qt[rs, ks_], kt[rs, ks_])
            attn = jnp.where(causal, attn, 0.0).astype(BF16)
            outs.append(_dot(attn, vb[rs, vs_]) + _dot_nt(qs[rs, ks_], states[h].astype(BF16)))
            states[h] = states[h] * dec[c][:, ks_] + _dot_tn(vb[rs, vs_], ks[rs, ks_])
        out_rows.append(jnp.concatenate(outs, axis=-1))
    for h in range(heads):
        st_ref[h] = states[h]
    return jnp.concatenate(out_rows, axis=0)


def _head_norm(o, g, heads, dv):
    outs = []
    for h in range(heads):
        oh = o[:, h * dv:(h + 1) * dv]
        r = lax.rsqrt(jnp.mean(oh * oh, axis=-1, keepdims=True) + EPS)
        outs.append(oh * r * g)
    return jnp.concatenate(outs, axis=-1)


def _mixer_kernel(x_ref, mod_ref, n1g_ref, win_ref, wa2_ref, ba_ref, glag_ref, hgg_ref, lbl_ref,
                  wout_ref, n2g_ref, wr_ref, br_ref,
                  x1_ref, h2_ref, meta_ref, cnt_ref,
                  proj_ref, o_ref, sg_ref, sh_ref, carry_ref):
    tile = x_ref.shape[1]
    bi = pl.program_id(0)
    si = pl.program_id(1)

    @pl.when(si == 0)
    def _():
        sg_ref[...] = jnp.zeros_like(sg_ref)
        sh_ref[...] = jnp.zeros_like(sh_ref)

    @pl.when((bi == 0) & (si == 0))
    def _():
        carry_ref[...] = jnp.zeros_like(carry_ref)

    mod = mod_ref[0]
    sh1, sc1, gt1 = mod[0:1], mod[1:2], mod[2:3]
    sh2, sc2 = mod[3:4], mod[4:5]

    x = x_ref[0]
    r1 = lax.rsqrt(jnp.mean(x * x, axis=-1, keepdims=True) + EPS)
    h = (x * r1 * n1g_ref[...]) * (1.0 + sc1) + sh1
    proj_ref[...] = _dot(h.astype(BF16), win_ref[...])

    a_logit = _dot(proj_ref[:, C_GA:C_GA + LANES].astype(BF16), wa2_ref[...]) + ba_ref[...]
    log_sig = jnp.minimum(a_logit, 0.0) - jnp.log(1.0 + jnp.exp(-jnp.abs(a_logit)))
    ld_g = log_sig * (1.0 / GLA_TAU)

    lbl = lbl_ref[...]
    lmax = jnp.max(lbl, axis=0, keepdims=True)
    lexp = jnp.exp(lbl - lmax)
    lb = lexp[0:1] / jnp.sum(lexp, axis=0, keepdims=True)

    row = lax.broadcasted_iota(jnp.int32, (CHUNK, CHUNK), 0)
    col = lax.broadcasted_iota(jnp.int32, (CHUNK, CHUNK), 1)
    causal = row >= col
    trow = lax.broadcasted_iota(jnp.int32, (tile, tile), 0)
    tcol = lax.broadcasted_iota(jnp.int32, (tile, tile), 1)
    same_chunk = (trow - tcol) <= (trow & (CHUNK - 1))
    blocktril = ((trow >= tcol) & same_chunk).astype(BF16)

    q = proj_ref[:, C_GQ:C_GK] * (GLA_DK ** -0.5)
    o_a = _recurrence(q, proj_ref[:, C_GK:C_GV], proj_ref[:, C_GV:C_GG], ld_g, sg_ref,
                      blocktril, causal, GLA_HEADS, GLA_DK, GLA_DV)
    o_a = _head_norm(o_a, glag_ref[...], GLA_HEADS, GLA_DV) * _silu(proj_ref[:, C_GG:C_GA])
    o_ref[:, 0:GLA_W] = o_a.astype(BF16)
    hq = proj_ref[:, C_HQ:C_HF] * (HG_DK ** -0.5)
    sig = jax.nn.sigmoid(proj_ref[:, C_HF:C_HI])
    f = lb + (1.0 - lb) * sig
    kh = (1.0 - lb) * (1.0 - sig)
    o_b = _recurrence(hq, kh, proj_ref[:, C_HI:C_HG], jnp.log(f), sh_ref,
                      blocktril, causal, HG_HEADS, HG_DK, HG_DV)
    o_b = _head_norm(o_b, hgg_ref[...], HG_HEADS, HG_DV) * _silu(proj_ref[:, C_HG:C_END])
    o_ref[:, GLA_W:GLA_W + HG_W] = o_b.astype(BF16)

    mix = _dot(o_ref[...], wout_ref[...])
    x1 = x_ref[0] + gt1 * mix
    x1_ref[0] = x1
    r2 = lax.rsqrt(jnp.mean(x1 * x1, axis=-1, keepdims=True) + EPS)
    h2 = (x1 * r2 * n2g_ref[...]) * (1.0 + sc2) + sh2
    h2_ref[0] = h2

    logits = _dot(h2.astype(BF16), wr_ref[...]) + br_ref[...]
    lane = lax.broadcasted_iota(jnp.int32, (tile, LANES), 1)
    neg = jnp.float32(-jnp.inf)
    big = jnp.int32(1 << 20)
    gl = jnp.where((lane >= N_EXPERTS) & (lane < N_EXPERTS + N_GROUPS), logits, neg)
    gmax = jnp.max(gl, axis=-1, keepdims=True)
    gidx = jnp.min(jnp.where(gl == gmax, lane, big), axis=-1, keepdims=True) - N_EXPERTS
    g_p = 1.0 / jnp.sum(jnp.exp(gl - gmax), axis=-1, keepdims=True)
    in_group = (lane >= gidx * EXPERTS_PER_GROUP) & (lane < (gidx + 1) * EXPERTS_PER_GROUP)
    el = jnp.where(in_group, logits, neg)
    m1 = jnp.max(el, axis=-1, keepdims=True)
    i1 = jnp.min(jnp.where(el == m1, lane, big), axis=-1, keepdims=True)
    el2 = jnp.where(lane == i1, neg, el)
    m2 = jnp.max(el2, axis=-1, keepdims=True)
    i2 = jnp.min(jnp.where(el2 == m2, lane, big), axis=-1, keepdims=True)
    t = jnp.exp(m2 - m1)
    w1 = g_p / (1.0 + t)
    w2 = g_p * t / (1.0 + t)

    sel1 = lane == i1
    sel2 = lane == i2
    onehot = (sel1 | sel2).astype(BF16)
    trow = lax.broadcasted_iota(jnp.int32, (tile, tile), 0)
    tcol = lax.broadcasted_iota(jnp.int32, (tile, tile), 1)
    strict = (trow > tcol).astype(BF16)
    cum = _dot(strict, onehot) + carry_ref[...]
    rank1 = jnp.sum(jnp.where(sel1, cum, 0.0), axis=-1, keepdims=True)
    rank2 = jnp.sum(jnp.where(sel2, cum, 0.0), axis=-1, keepdims=True)
    carry_ref[...] = carry_ref[...] + jnp.sum(onehot.astype(F32), axis=0, keepdims=True)
    cnt_ref[...] = carry_ref[...]

    meta = jnp.where(lane == 0, i1.astype(F32), 0.0)
    meta = jnp.where(lane == 1, i2.astype(F32), meta)
    meta = jnp.where(lane == 2, w1, meta)
    meta = jnp.where(lane == 3, w2, meta)
    meta = jnp.where(lane == 4, rank1, meta)
    meta = jnp.where(lane == 5, rank2, meta)
    meta_ref[0] = meta


def _mixer(x, modall, n1g, win, wa2, ba, glag, hgg, lbl, wout, n2g, wr, br):
    bsz, seq, dm = x.shape
    tile = min(MIX_TILE, seq)
    grid = (bsz, seq // tile)
    const = lambda shape: pl.BlockSpec(shape, lambda b, s: (0,) * len(shape))
    tok = lambda width: pl.BlockSpec((1, tile, width), lambda b, s: (b, s, 0))
    return pl.pallas_call(
        _mixer_kernel,
        grid=grid,
        in_specs=[tok(dm),
                  pl.BlockSpec((1, 8, dm), lambda b, s: (b, 0, 0)),
                  const((1, dm)), const((dm, C_END)), const((LANES, GLA_QK)), const((1, GLA_QK)),
                  const((1, GLA_DV)), const((1, HG_DV)), const((2, HG_W)),
                  const((dm, dm)), const((1, dm)), const((dm, LANES)), const((1, LANES))],
        out_specs=[tok(dm), tok(dm), tok(LANES), const((1, LANES))],
        out_shape=[jax.ShapeDtypeStruct((bsz, seq, dm), F32),
                   jax.ShapeDtypeStruct((bsz, seq, dm), F32),
                   jax.ShapeDtypeStruct((bsz, seq, LANES), F32),
                   jax.ShapeDtypeStruct((1, LANES), F32)],
        scratch_shapes=[pltpu.VMEM((tile, C_END), F32),
                        pltpu.VMEM((tile, dm), BF16),
                        pltpu.VMEM((GLA_HEADS, GLA_DV, GLA_DK), F32),
                        pltpu.VMEM((HG_HEADS, HG_DV, HG_DK), F32),
                        pltpu.VMEM((1, LANES), F32)],
        compiler_params=pltpu.CompilerParams(
            dimension_semantics=("arbitrary", "arbitrary"), vmem_limit_bytes=VMEM_LIMIT),
        name="mixer",
    )(x, modall, n1g, win, wa2, ba, glag, hgg, lbl, wout, n2g, wr, br)


def _dispatch_kernel(tails_ref, slot_ref, h2_ref, xs_ref, zero_ref, sem, zsem):
    tile = h2_ref.shape[0]

    @pl.when(pl.program_id(0) == 0)
    def _():
        zero_ref[...] = jnp.zeros_like(zero_ref)

        def zero_copy(e):
            return pltpu.make_async_copy(
                zero_ref, xs_ref.at[pl.ds(pl.multiple_of(tails_ref[e], 8), EXP_TILE), :], zsem)

        for e in range(N_EXPERTS):
            @pl.when(tails_ref[e] >= 0)
            def _():
                zero_copy(e).start()
        for e in range(N_EXPERTS):
            @pl.when(tails_ref[e] >= 0)
            def _():
                zero_copy(e).wait()

    def issue(t, carry):
        for j in range(2):
            pltpu.make_async_copy(h2_ref.at[pl.ds(t, 1), :],
                                  xs_ref.at[pl.ds(slot_ref[2 * t + j], 1), :], sem).start()
        return carry

    lax.fori_loop(0, tile, issue, 0, unroll=8)
    for _ in range(2):
        pltpu.make_async_copy(h2_ref, h2_ref, sem).wait()


def _dispatch(tails, slots, h2, n_rows):
    n, dm = h2.shape
    tile = min(TOK_TILE, n)
    return pl.pallas_call(
        _dispatch_kernel,
        grid_spec=pltpu.PrefetchScalarGridSpec(
            num_scalar_prefetch=1,
            grid=(n // tile,),
            in_specs=[pl.BlockSpec((2 * tile,), lambda i, tl: (i,), memory_space=pltpu.SMEM),
                      pl.BlockSpec((tile, dm), lambda i, tl: (i, 0))],
            out_specs=pl.BlockSpec(memory_space=pl.ANY),
            scratch_shapes=[pltpu.VMEM((EXP_TILE, dm), F32),
                            pltpu.SemaphoreType.DMA(()), pltpu.SemaphoreType.DMA(())]),
        out_shape=jax.ShapeDtypeStruct((n_rows, dm), F32),
        compiler_params=pltpu.CompilerParams(
            dimension_semantics=("arbitrary",), vmem_limit_bytes=VMEM_LIMIT),
        name="dispatch",
    )(tails, slots, h2)


def _experts_kernel(te_ref, nu_ref, xs_ref, wg_ref, wu_ref, wd_ref, ys_ref):
    i = pl.program_id(0)

    @pl.when(i < nu_ref[0])
    def _():
        xb = xs_ref[...].astype(BF16)
        g = _dot(xb, wg_ref[0].astype(BF16))
        u = _dot(xb, wu_ref[0].astype(BF16))
        a = (_silu(g) * u).astype(BF16)
        ys_ref[...] = _dot(a, wd_ref[0].astype(BF16))


def _experts(tile_expert, n_used, xs, wg, wu, wd, n_tiles):
    rows, dm = xs.shape
    tm = EXP_TILE
    ff = wg.shape[-1]

    def row_map(i, te, nu):
        return (jnp.minimum(i, nu[0] - 1), 0)

    def out_map(i, te, nu):
        return (jnp.where(i < nu[0], i, n_tiles), 0)

    return pl.pallas_call(
        _experts_kernel,
        grid_spec=pltpu.PrefetchScalarGridSpec(
            num_scalar_prefetch=2,
            grid=(n_tiles,),
            in_specs=[pl.BlockSpec((tm, dm), row_map),
                      pl.BlockSpec((1, dm, ff), lambda i, te, nu: (te[i], 0, 0)),
                      pl.BlockSpec((1, dm, ff), lambda i, te, nu: (te[i], 0, 0)),
                      pl.BlockSpec((1, ff, dm), lambda i, te, nu: (te[i], 0, 0))],
            out_specs=pl.BlockSpec((tm, dm), out_map)),
        out_shape=jax.ShapeDtypeStruct(((n_tiles + 1) * tm, dm), F32),
        compiler_params=pltpu.CompilerParams(
            dimension_semantics=("arbitrary",), vmem_limit_bytes=VMEM_LIMIT),
        name="experts",
    )(tile_expert, n_used, xs, wg, wu, wd)


def _combine_kernel(slot_ref, x1_ref, meta_ref, mod_ref, nfg_ref, ys_ref, out_ref, buf_ref, sem):
    tile = x1_ref.shape[1]

    def issue(t, carry):
        for j in range(2):
            pltpu.make_async_copy(ys_ref.at[pl.ds(slot_ref[2 * t + j], 1), :],
                                  buf_ref.at[j, pl.ds(t, 1), :], sem).start()
        return carry

    lax.fori_loop(0, tile, issue, 0, unroll=8)
    for j in range(2):
        pltpu.make_async_copy(buf_ref.at[j], buf_ref.at[j], sem).wait()

    mod = mod_ref[0]
    gt2, shf, scf = mod[5:6], mod[6:7], mod[7:8]
    meta = meta_ref[0]
    w1 = meta[:, 2:3]
    w2 = meta[:, 3:4]
    ffn = w1 * buf_ref[0] + w2 * buf_ref[1]
    x2 = x1_ref[0] + gt2 * ffn
    r = lax.rsqrt(jnp.mean(x2 * x2, axis=-1, keepdims=True) + EPS)
    out_ref[0] = (x2 * r * nfg_ref[...]) * (1.0 + scf) + shf


def _combine(slots, x1, meta, modall, nfg, ys):
    bsz, seq, dm = x1.shape
    tile = min(TOK_TILE, seq)
    spt = seq // tile
    return pl.pallas_call(
        _combine_kernel,
        grid=(bsz, spt),
        in_specs=[pl.BlockSpec((2 * tile,), lambda b, s: (b * spt + s,), memory_space=pltpu.SMEM),
                  pl.BlockSpec((1, tile, dm), lambda b, s: (b, s, 0)),
                  pl.BlockSpec((1, tile, LANES), lambda b, s: (b, s, 0)),
                  pl.BlockSpec((1, 8, dm), lambda b, s: (b, 0, 0)),
                  pl.BlockSpec((1, dm), lambda b, s: (0, 0)),
                  pl.BlockSpec(memory_space=pl.ANY)],
        out_specs=pl.BlockSpec((1, tile, dm), lambda b, s: (b, s, 0)),
        out_shape=jax.ShapeDtypeStruct((bsz, seq, dm), F32),
        scratch_shapes=[pltpu.VMEM((2, tile, dm), F32), pltpu.SemaphoreType.DMA(())],
        compiler_params=pltpu.CompilerParams(
            dimension_semantics=("arbitrary", "arbitrary"), vmem_limit_bytes=VMEM_LIMIT),
        name="combine",
    )(slots, x1, meta, modall, nfg, ys)


def kernel(x, c, w_ada, b_ada, norm1_g, w_in, gla_w_a2, gla_b_a, gla_norm_g, hg_norm_g, hg_lb_logits,
           w_out, norm2_g, w_rg, b_rg, w_re, b_re, w_exp_gate, w_exp_up, w_exp_down, w_ada_final,
           b_ada_final, norm_f_g):
    bsz, seq, dm = x.shape
    n_tok = bsz * seq

    mod = _ada(c, w_ada[0], b_ada[0], 1536)
    mod_f = _ada(c, w_ada_final, b_ada_final, 1024)
    modall = jnp.concatenate([mod.reshape(bsz, 6, dm), mod_f.reshape(bsz, 2, dm)], axis=1)

    wi = w_in[0]
    ga0 = 2 * GLA_QK + 2 * GLA_W
    win = jnp.concatenate([wi[:, :ga0],
                           jnp.pad(wi[:, ga0:ga0 + GLA_RANK], ((0, 0), (0, LANES - GLA_RANK))),
                           wi[:, ga0 + GLA_RANK:]], axis=1).astype(BF16)
    wa2 = jnp.pad(gla_w_a2[0], ((0, LANES - GLA_RANK), (0, 0))).astype(BF16)
    pad_r = LANES - N_EXPERTS - N_GROUPS
    wr = jnp.pad(jnp.concatenate([w_re[0], w_rg[0]], axis=1), ((0, 0), (0, pad_r))).astype(BF16)
    br = jnp.pad(jnp.concatenate([b_re[0], b_rg[0]]), (0, pad_r)).reshape(1, LANES)

    x1, h2, meta, cnt = _mixer(
        x, modall, norm1_g[0].reshape(1, dm), win, wa2, gla_b_a[0].reshape(1, GLA_QK),
        gla_norm_g[0].reshape(1, GLA_DV), hg_norm_g[0].reshape(1, HG_DV), hg_lb_logits,
        w_out[0].astype(BF16), norm2_g[0].reshape(1, dm), wr, br)

    tm = EXP_TILE
    n_tiles = (2 * n_tok) // tm + N_EXPERTS
    counts = cnt[0, :N_EXPERTS].astype(jnp.int32)
    padded = ((counts + tm - 1) // tm) * tm
    ends = jnp.cumsum(padded)
    offs = ends - padded
    tails = jnp.where(padded > 0, ends - tm, -1).astype(jnp.int32)
    meta2 = meta.reshape(n_tok, LANES)
    ids = meta2[:, 0:2].astype(jnp.int32)
    ranks = meta2[:, 4:6].astype(jnp.int32)
    expert_iota = jnp.arange(N_EXPERTS, dtype=jnp.int32)
    offs_of = jnp.sum(jnp.where(ids[..., None] == expert_iota, offs, 0), axis=-1)
    slots = (offs_of + ranks).reshape(2 * n_tok).astype(jnp.int32)
    n_used = (ends[-1] // tm).astype(jnp.int32).reshape(1)
    tile_start = jnp.arange(n_tiles, dtype=jnp.int32) * tm
    tile_start = jnp.minimum(tile_start, ends[-1] - tm)
    tile_expert = jnp.sum(tile_start[:, None] >= ends[None, :], axis=1).astype(jnp.int32)

    xs = _dispatch(tails, slots, h2.reshape(n_tok, dm), n_tiles * tm)
    ys = _experts(tile_expert, n_used, xs, w_exp_gate[0], w_exp_up[0], w_exp_down[0], n_tiles)
    return _combine(slots, x1, meta, modall, norm_f_g.reshape(1, dm), ys)
```

```python
import functools

import jax
import jax.numpy as jnp
from jax import lax
from jax.experimental import pallas as pl
from jax.experimental.pallas import tpu as pltpu

F32 = jnp.float32
BF16 = jnp.bfloat16

D_MODEL = 1024
GLA_HEADS, GLA_DK, GLA_DV, GLA_RANK, GLA_TAU = 4, 64, 128, 16, 16.0
HG_HEADS, HG_DK, HG_DV = 4, 128, 128
GLA_QK = GLA_HEADS * GLA_DK
GLA_W = GLA_HEADS * GLA_DV
HG_W = HG_HEADS * HG_DK
CHUNK = 64
N_GROUPS, EXPERTS_PER_GROUP, N_EXPERTS, EXPERT_FF = 4, 8, 32, 256
EPS = 1e-6
LANES = 128

C_GQ, C_GK, C_GV, C_GG, C_GA = 0, 256, 512, 1024, 1536
C_HQ, C_HF, C_HI, C_HG, C_END = 1664, 2176, 2688, 3200, 3712

TOK_TILE = 256
EXP_TILE = 512
ROW_GRAIN = 8
LOCAL_ROWS = 2 * TOK_TILE + N_EXPERTS * ROW_GRAIN
MAX_PIECES = LOCAL_ROWS // ROW_GRAIN
EXP_CLAMP = 80.0
VMEM_LIMIT = 56 * 1024 * 1024


def _silu(t):
    return t * jax.nn.sigmoid(t)


def _dot(a, b):
    return jnp.dot(a, b, preferred_element_type=F32)


def _dot_nt(a, b):
    return lax.dot_general(a, b, (((1,), (1,)), ((), ())), preferred_element_type=F32)


def _dot_tn(a, b):
    return lax.dot_general(a, b, (((0,), (0,)), ((), ())), preferred_element_type=F32)


def _ada_kernel(c_ref, w_ref, b_ref, o_ref):
    ca = _silu(c_ref[...]).astype(BF16)
    o_ref[...] = _dot(ca, w_ref[...].astype(BF16)) + b_ref[...]


def _ada(c, w, b, tn):
    bsz, dm = c.shape
    n = w.shape[1]
    return pl.pallas_call(
        _ada_kernel,
        grid=(n // tn,),
        in_specs=[pl.BlockSpec((bsz, dm), lambda j: (0, 0)),
                  pl.BlockSpec((dm, tn), lambda j: (0, j)),
                  pl.BlockSpec((1, tn), lambda j: (0, j))],
        out_specs=pl.BlockSpec((bsz, tn), lambda j: (0, j)),
        out_shape=jax.ShapeDtypeStruct((bsz, n), F32),
        compiler_params=pltpu.CompilerParams(vmem_limit_bytes=VMEM_LIMIT),
        name="ada",
    )(c, w, b.reshape(1, n))


def _split2(t):
    hi = t.astype(BF16)
    lo = (t - hi.astype(F32)).astype(BF16)
    return hi, lo


def _recurrence(q, k, v, ld, st_ref, blocktril, causal, heads, dk, dv):
    t, w = q.shape
    nc = t // CHUNK
    ld_hi, ld_lo = _split2(ld)
    b = (_dot(blocktril, ld_hi) + _dot(blocktril, ld_lo)).reshape(nc, CHUNK, w)
    bm = b[:, CHUNK // 2 - 1:CHUNK // 2, :]
    bl = b[:, CHUNK - 1:CHUNK, :]
    qt = q.reshape(nc, CHUNK, w) * jnp.exp(jnp.minimum(b - bm, EXP_CLAMP))
    kt = k.reshape(nc, CHUNK, w) * jnp.exp(jnp.minimum(bm - b, EXP_CLAMP))
    qs = (qt * jnp.exp(bm)).astype(BF16).reshape(t, w)
    ks = (kt * jnp.exp(bl - bm)).astype(BF16).reshape(t, w)
    qt = qt.astype(BF16).reshape(t, w)
    kt = kt.astype(BF16).reshape(t, w)
    dec = jnp.exp(bl)
    vb = v.astype(BF16)
    states = [st_ref[h] for h in range(heads)]
    out_rows = []
    for c in range(nc):
        rs = slice(c * CHUNK, (c + 1) * CHUNK)
        outs = []
        for h in range(heads):
            ks_ = slice(h * dk, (h + 1) * dk)
            vs_ = slice(h * dv, (h + 1) * dv)
            attn = _dot_nt(qt[rs, ks_], kt[rs, ks_])
            attn = jnp.where(causal, attn, 0.0).astype(BF16)
            outs.append(_dot(attn, vb[rs, vs_]) + _dot_nt(qs[rs, ks_], states[h].astype(BF16)))
            states[h] = states[h] * dec[c][:, ks_] + _dot_tn(vb[rs, vs_], ks[rs, ks_])
        out_rows.append(jnp.concatenate(outs, axis=-1))
    for h in range(heads):
        st_ref[h] = states[h]
    return jnp.concatenate(out_rows, axis=0)


def _head_norm(o, g, heads, dv):
    outs = []
    for h in range(heads):
        oh = o[:, h * dv:(h + 1) * dv]
        r = lax.rsqrt(jnp.mean(oh * oh, axis=-1, keepdims=True) + EPS)
        outs.append(oh * r * g)
    return jnp.concatenate(outs, axis=-1)


def _mixer_kernel(x_ref, mod_ref, n1g_ref, win_ref, wa2_ref, ba_ref, glag_ref, hgg_ref, lbl_ref,
                  wout_ref, n2g_ref, wr_ref, br_ref,
                  x1_ref, h2_ref, meta_ref, info_ref, cnt_ref,
                  proj_ref, o_ref, sg_ref, sh_ref, carry_ref):
    tile = x_ref.shape[1]
    bi = pl.program_id(0)
    si = pl.program_id(1)

    @pl.when(si == 0)
    def _():
        sg_ref[...] = jnp.zeros_like(sg_ref)
        sh_ref[...] = jnp.zeros_like(sh_ref)

    @pl.when((bi == 0) & (si == 0))
    def _():
        carry_ref[...] = jnp.zeros_like(carry_ref)

    mod = mod_ref[0]
    sh1, sc1, gt1 = mod[0:1], mod[1:2], mod[2:3]
    sh2, sc2 = mod[3:4], mod[4:5]

    x = x_ref[0]
    r1 = lax.rsqrt(jnp.mean(x * x, axis=-1, keepdims=True) + EPS)
    h = (x * r1 * n1g_ref[...]) * (1.0 + sc1) + sh1
    proj_ref[...] = _dot(h.astype(BF16), win_ref[...])

    a_logit = _dot(proj_ref[:, C_GA:C_GA + LANES].astype(BF16), wa2_ref[...]) + ba_ref[...]
    log_sig = jnp.minimum(a_logit, 0.0) - jnp.log(1.0 + jnp.exp(-jnp.abs(a_logit)))
    ld_g = log_sig * (1.0 / GLA_TAU)

    lbl = lbl_ref[...]
    lmax = jnp.max(lbl, axis=0, keepdims=True)
    lexp = jnp.exp(lbl - lmax)
    lb = lexp[0:1] / jnp.sum(lexp, axis=0, keepdims=True)

    row = lax.broadcasted_iota(jnp.int32, (CHUNK, CHUNK), 0)
    col = lax.broadcasted_iota(jnp.int32, (CHUNK, CHUNK), 1)
    causal = row >= col
    trow = lax.broadcasted_iota(jnp.int32, (tile, tile), 0)
    tcol = lax.broadcasted_iota(jnp.int32, (tile, tile), 1)
    same_chunk = (trow - tcol) <= (trow & (CHUNK - 1))
    blocktril = ((trow >= tcol) & same_chunk).astype(BF16)

    q = proj_ref[:, C_GQ:C_GK] * (GLA_DK ** -0.5)
    o_a = _recurrence(q, proj_ref[:, C_GK:C_GV], proj_ref[:, C_GV:C_GG], ld_g, sg_ref,
                      blocktril, causal, GLA_HEADS, GLA_DK, GLA_DV)
    o_a = _head_norm(o_a, glag_ref[...], GLA_HEADS, GLA_DV) * _silu(proj_ref[:, C_GG:C_GA])
    o_ref[:, 0:GLA_W] = o_a.astype(BF16)
    hq = proj_ref[:, C_HQ:C_HF] * (HG_DK ** -0.5)
    sig = jax.nn.sigmoid(proj_ref[:, C_HF:C_HI])
    f = lb + (1.0 - lb) * sig
    kh = (1.0 - lb) * (1.0 - sig)
    o_b = _recurrence(hq, kh, proj_ref[:, C_HI:C_HG], jnp.log(f), sh_ref,
                      blocktril, causal, HG_HEADS, HG_DK, HG_DV)
    o_b = _head_norm(o_b, hgg_ref[...], HG_HEADS, HG_DV) * _silu(proj_ref[:, C_HG:C_END])
    o_ref[:, GLA_W:GLA_W + HG_W] = o_b.astype(BF16)

    mix = _dot(o_ref[...], wout_ref[...])
    x1 = x_ref[0] + gt1 * mix
    x1_ref[0] = x1
    r2 = lax.rsqrt(jnp.mean(x1 * x1, axis=-1, keepdims=True) + EPS)
    h2 = (x1 * r2 * n2g_ref[...]) * (1.0 + sc2) + sh2
    h2b = h2.astype(BF16)
    h2_ref[0] = h2b

    logits = _dot(h2b, wr_ref[...]) + br_ref[...]
    lane = lax.broadcasted_iota(jnp.int32, (tile, LANES), 1)
    neg = jnp.float32(-jnp.inf)
    big = jnp.int32(1 << 20)
    gl = jnp.where((lane >= N_EXPERTS) & (lane < N_EXPERTS + N_GROUPS), logits, neg)
    gmax = jnp.max(gl, axis=-1, keepdims=True)
    gidx = jnp.min(jnp.where(gl == gmax, lane, big), axis=-1, keepdims=True) - N_EXPERTS
    g_p = 1.0 / jnp.sum(jnp.exp(gl - gmax), axis=-1, keepdims=True)
    in_group = (lane >= gidx * EXPERTS_PER_GROUP) & (lane < (gidx + 1) * EXPERTS_PER_GROUP)
    el = jnp.where(in_group, logits, neg)
    m1 = jnp.max(el, axis=-1, keepdims=True)
    i1 = jnp.min(jnp.where(el == m1, lane, big), axis=-1, keepdims=True)
    el2 = jnp.where(lane == i1, neg, el)
    m2 = jnp.max(el2, axis=-1, keepdims=True)
    i2 = jnp.min(jnp.where(el2 == m2, lane, big), axis=-1, keepdims=True)
    t = jnp.exp(m2 - m1)
    w1 = g_p / (1.0 + t)
    w2 = g_p * t / (1.0 + t)

    sel1 = lane == i1
    sel2 = lane == i2
    onehot = (sel1 | sel2).astype(BF16)
    trow = lax.broadcasted_iota(jnp.int32, (tile, tile), 0)
    tcol = lax.broadcasted_iota(jnp.int32, (tile, tile), 1)
    strict = (trow > tcol).astype(BF16)
    lcum = _dot(strict, onehot)
    cnt = jnp.sum(onehot.astype(F32), axis=0, keepdims=True)
    run = jnp.floor((cnt + (ROW_GRAIN - 1)) * (1.0 / ROW_GRAIN)) * ROW_GRAIN
    erow = lax.broadcasted_iota(jnp.int32, (LANES, LANES), 0)
    ecol = lax.broadcasted_iota(jnp.int32, (LANES, LANES), 1)
    before = (erow < ecol).astype(BF16)
    run8 = jnp.broadcast_to(run, (8, LANES))
    loff = _dot(run8.astype(BF16), before)
    pos = lcum + loff[0:1]
    p1 = jnp.sum(jnp.where(sel1, pos, 0.0), axis=-1, keepdims=True)
    p2 = jnp.sum(jnp.where(sel2, pos, 0.0), axis=-1, keepdims=True)

    srow = lax.broadcasted_iota(jnp.int32, (8, LANES), 0)
    info = jnp.where(srow == 0, run8, jnp.where(srow == 1, carry_ref[...], jnp.where(srow == 2, loff, 0.0)))
    info_ref[0] = info
    carry_ref[...] = carry_ref[...] + run
    cnt_ref[...] = carry_ref[...]

    meta = jnp.where(lane == 0, p1, 0.0)
    meta = jnp.where(lane == 1, p2, meta)
    meta = jnp.where(lane == 2, w1, meta)
    meta = jnp.where(lane == 3, w2, meta)
    meta_ref[0] = meta


def _mixer(x, modall, n1g, win, wa2, ba, glag, hgg, lbl, wout, n2g, wr, br):
    bsz, seq, dm = x.shape
    tile = min(TOK_TILE, seq)
    spt = seq // tile
    grid = (bsz, spt)
    const = lambda shape: pl.BlockSpec(shape, lambda b, s: (0,) * len(shape))
    tok = lambda width: pl.BlockSpec((1, tile, width), lambda b, s: (b, s, 0))
    return pl.pallas_call(
        _mixer_kernel,
        grid=grid,
        in_specs=[tok(dm),
                  pl.BlockSpec((1, 8, dm), lambda b, s: (b, 0, 0)),
                  const((1, dm)), const((dm, C_END)), const((LANES, GLA_QK)), const((1, GLA_QK)),
                  const((1, GLA_DV)), const((1, HG_DV)), const((2, HG_W)),
                  const((dm, dm)), const((1, dm)), const((dm, LANES)), const((1, LANES))],
        out_specs=[tok(dm), tok(dm), tok(LANES),
                   pl.BlockSpec((1, 8, LANES), lambda b, s: (b * spt + s, 0, 0)),
                   const((1, LANES))],
        out_shape=[jax.ShapeDtypeStruct((bsz, seq, dm), F32),
                   jax.ShapeDtypeStruct((bsz, seq, dm), BF16),
                   jax.ShapeDtypeStruct((bsz, seq, LANES), F32),
                   jax.ShapeDtypeStruct((bsz * spt, 8, LANES), F32),
                   jax.ShapeDtypeStruct((1, LANES), F32)],
        scratch_shapes=[pltpu.VMEM((tile, C_END), F32),
                        pltpu.VMEM((tile, dm), BF16),
                        pltpu.VMEM((GLA_HEADS, GLA_DV, GLA_DK), F32),
                        pltpu.VMEM((HG_HEADS, HG_DV, HG_DK), F32),
                        pltpu.VMEM((1, LANES), F32)],
        compiler_params=pltpu.CompilerParams(
            dimension_semantics=("arbitrary", "arbitrary"), vmem_limit_bytes=VMEM_LIMIT),
        name="mixer",
    )(x, modall, n1g, win, wa2, ba, glag, hgg, lbl, wout, n2g, wr, br)


def _local_positions(meta, rows):
    lane = lax.broadcasted_iota(jnp.int32, (meta.shape[0], rows), 1)
    p1 = meta[:, 0:1].astype(jnp.int32)
    p2 = meta[:, 1:2].astype(jnp.int32)
    return lane == p1, lane == p2


def _dispatch_kernel(tails_ref, np_ref, dst_ref, meta_ref, h2_ref, xs_ref, xl_ref, zero_ref, sem, zsem):
    @pl.when(pl.program_id(0) == 0)
    def _():
        zero_ref[...] = jnp.zeros_like(zero_ref)

        def zero_copy(e):
            return pltpu.make_async_copy(
                zero_ref, xs_ref.at[pl.ds(pl.multiple_of(tails_ref[e], 8), EXP_TILE), :], zsem)

        for e in range(N_EXPERTS):
            @pl.when(tails_ref[e] >= 0)
            def _():
                zero_copy(e).start()
        for e in range(N_EXPERTS):
            @pl.when(tails_ref[e] >= 0)
            def _():
                zero_copy(e).wait()

    sel1, sel2 = _local_positions(meta_ref[...], LOCAL_ROWS)
    xl_ref[...] = _dot_tn((sel1 | sel2).astype(BF16), h2_ref[...])

    def piece(j):
        src = xl_ref.at[pl.ds(pl.multiple_of(j * ROW_GRAIN, ROW_GRAIN), ROW_GRAIN), :]
        dst = xs_ref.at[pl.ds(pl.multiple_of(dst_ref[j], ROW_GRAIN), ROW_GRAIN), :]
        return pltpu.make_async_copy(src, dst, sem)

    def issue(j, carry):
        piece(j).start()
        return carry

    def drain(j, carry):
        piece(j).wait()
        return carry

    n_pieces = np_ref[pl.program_id(0)]
    lax.fori_loop(0, n_pieces, issue, 0)
    lax.fori_loop(0, n_pieces, drain, 0)


def _dispatch(tails, n_pieces, dst, meta, h2, n_rows):
    n, dm = h2.shape
    tile = TOK_TILE
    return pl.pallas_call(
        _dispatch_kernel,
        grid_spec=pltpu.PrefetchScalarGridSpec(
            num_scalar_prefetch=2,
            grid=(n // tile,),
            in_specs=[pl.BlockSpec((LANES,), lambda i, tl, npc: (i,), memory_space=pltpu.SMEM),
                      pl.BlockSpec((tile, LANES), lambda i, tl, npc: (i, 0)),
                      pl.BlockSpec((tile, dm), lambda i, tl, npc: (i, 0))],
            out_specs=pl.BlockSpec(memory_space=pl.ANY),
            scratch_shapes=[pltpu.VMEM((LOCAL_ROWS, dm), F32),
                            pltpu.VMEM((EXP_TILE, dm), F32),
                            pltpu.SemaphoreType.DMA(()), pltpu.SemaphoreType.DMA(())]),
        out_shape=jax.ShapeDtypeStruct((n_rows, dm), F32),
        compiler_params=pltpu.CompilerParams(
            dimension_semantics=("arbitrary",), vmem_limit_bytes=VMEM_LIMIT),
        name="dispatch",
    )(tails, n_pieces, dst, meta, h2)


def _experts_kernel(te_ref, nu_ref, xs_ref, wg_ref, wu_ref, wd_ref, ys_ref):
    i = pl.program_id(0)

    @pl.when(i < nu_ref[0])
    def _():
        xb = xs_ref[...].astype(BF16)
        g = _dot(xb, wg_ref[0].astype(BF16))
        u = _dot(xb, wu_ref[0].astype(BF16))
        a = (_silu(g) * u).astype(BF16)
        ys_ref[...] = _dot(a, wd_ref[0].astype(BF16))


def _experts(tile_expert, n_used, xs, wg, wu, wd, n_tiles):
    rows, dm = xs.shape
    tm = EXP_TILE
    ff = wg.shape[-1]

    def row_map(i, te, nu):
        return (jnp.minimum(i, nu[0] - 1), 0)

    def out_map(i, te, nu):
        return (jnp.where(i < nu[0], i, n_tiles), 0)

    return pl.pallas_call(
        _experts_kernel,
        grid_spec=pltpu.PrefetchScalarGridSpec(
            num_scalar_prefetch=2,
            grid=(n_tiles,),
            in_specs=[pl.BlockSpec((tm, dm), row_map),
                      pl.BlockSpec((1, dm, ff), lambda i, te, nu: (te[i], 0, 0)),
                      pl.BlockSpec((1, dm, ff), lambda i, te, nu: (te[i], 0, 0)),
                      pl.BlockSpec((1, ff, dm), lambda i, te, nu: (te[i], 0, 0))],
            out_specs=pl.BlockSpec((tm, dm), out_map)),
        out_shape=jax.ShapeDtypeStruct(((n_tiles + 1) * tm, dm), F32),
        compiler_params=pltpu.CompilerParams(
            dimension_semantics=("arbitrary",), vmem_limit_bytes=VMEM_LIMIT),
        name="experts",
    )(tile_expert, n_used, xs, wg, wu, wd)


def _combine_kernel(np_ref, dst_ref, x1_ref, meta_ref, mod_ref, nfg_ref, ys_ref, out_ref, yl_ref, sem):
    first = (pl.program_id(0) == 0) & (pl.program_id(1) == 0)

    @pl.when(first)
    def _():
        yl_ref[...] = jnp.zeros_like(yl_ref)

    def piece(j):
        src = ys_ref.at[pl.ds(pl.multiple_of(dst_ref[j], ROW_GRAIN), ROW_GRAIN), :]
        dst = yl_ref.at[pl.ds(pl.multiple_of(j * ROW_GRAIN, ROW_GRAIN), ROW_GRAIN), :]
        return pltpu.make_async_copy(src, dst, sem)

    def issue(j, carry):
        piece(j).start()
        return carry

    def drain(j, carry):
        piece(j).wait()
        return carry

    n_pieces = np_ref[pl.program_id(0) * pl.num_programs(1) + pl.program_id(1)]
    lax.fori_loop(0, n_pieces, issue, 0)
    lax.fori_loop(0, n_pieces, drain, 0)

    mod = mod_ref[0]
    gt2, shf, scf = mod[5:6], mod[6:7], mod[7:8]
    meta = meta_ref[0]
    sel1, sel2 = _local_positions(meta, LOCAL_ROWS)
    wsel = jnp.where(sel1, meta[:, 2:3], 0.0) + jnp.where(sel2, meta[:, 3:4], 0.0)
    ffn = _dot(wsel.astype(BF16), yl_ref[...].astype(BF16))
    x2 = x1_ref[0] + gt2 * ffn
    r = lax.rsqrt(jnp.mean(x2 * x2, axis=-1, keepdims=True) + EPS)
    out_ref[0] = (x2 * r * nfg_ref[...]) * (1.0 + scf) + shf


def _combine(n_pieces, dst, x1, meta, modall, nfg, ys):
    bsz, seq, dm = x1.shape
    tile = TOK_TILE
    spt = seq // tile
    return pl.pallas_call(
        _combine_kernel,
        grid_spec=pltpu.PrefetchScalarGridSpec(
            num_scalar_prefetch=1,
            grid=(bsz, spt),
            in_specs=[pl.BlockSpec((LANES,), lambda b, s, npc: (b * spt + s,), memory_space=pltpu.SMEM),
                      pl.BlockSpec((1, tile, dm), lambda b, s, npc: (b, s, 0)),
                      pl.BlockSpec((1, tile, LANES), lambda b, s, npc: (b, s, 0)),
                      pl.BlockSpec((1, 8, dm), lambda b, s, npc: (b, 0, 0)),
                      pl.BlockSpec((1, dm), lambda b, s, npc: (0, 0)),
                      pl.BlockSpec(memory_space=pl.ANY)],
            out_specs=pl.BlockSpec((1, tile, dm), lambda b, s, npc: (b, s, 0)),
            scratch_shapes=[pltpu.VMEM((LOCAL_ROWS, dm), F32), pltpu.SemaphoreType.DMA(())]),
        out_shape=jax.ShapeDtypeStruct((bsz, seq, dm), F32),
        compiler_params=pltpu.CompilerParams(
            dimension_semantics=("arbitrary", "arbitrary"), vmem_limit_bytes=VMEM_LIMIT),
        name="combine",
    )(n_pieces, dst, x1, meta, modall, nfg, ys)


def kernel(x, c, w_ada, b_ada, norm1_g, w_in, gla_w_a2, gla_b_a, gla_norm_g, hg_norm_g, hg_lb_logits,
           w_out, norm2_g, w_rg, b_rg, w_re, b_re, w_exp_gate, w_exp_up, w_exp_down, w_ada_final,
           b_ada_final, norm_f_g):
    bsz, seq, dm = x.shape
    n_tok = bsz * seq

    mod = _ada(c, w_ada[0], b_ada[0], 1536)
    mod_f = _ada(c, w_ada_final, b_ada_final, 1024)
    modall = jnp.concatenate([mod.reshape(bsz, 6, dm), mod_f.reshape(bsz, 2, dm)], axis=1)

    wi = w_in[0]
    ga0 = 2 * GLA_QK + 2 * GLA_W
    win = jnp.concatenate([wi[:, :ga0],
                           jnp.pad(wi[:, ga0:ga0 + GLA_RANK], ((0, 0), (0, LANES - GLA_RANK))),
                           wi[:, ga0 + GLA_RANK:]], axis=1).astype(BF16)
    wa2 = jnp.pad(gla_w_a2[0], ((0, LANES - GLA_RANK), (0, 0))).astype(BF16)
    pad_r = LANES - N_EXPERTS - N_GROUPS
    wr = jnp.pad(jnp.concatenate([w_re[0], w_rg[0]], axis=1), ((0, 0), (0, pad_r))).astype(BF16)
    br = jnp.pad(jnp.concatenate([b_re[0], b_rg[0]]), (0, pad_r)).reshape(1, LANES)

    assert seq % TOK_TILE == 0
    x1, h2, meta, info, total = _mixer(
        x, modall, norm1_g[0].reshape(1, dm), win, wa2, gla_b_a[0].reshape(1, GLA_QK),
        gla_norm_g[0].reshape(1, GLA_DV), hg_norm_g[0].reshape(1, HG_DV), hg_lb_logits,
        w_out[0].astype(BF16), norm2_g[0].reshape(1, dm), wr, br)

    tm = EXP_TILE
    n_tok_tiles = n_tok // TOK_TILE
    n_tiles = (2 * n_tok + n_tok_tiles * N_EXPERTS * (ROW_GRAIN - 1)) // tm + N_EXPERTS
    i32 = jnp.int32
    run = info[:, 0, :N_EXPERTS].astype(i32)
    before = info[:, 1, :N_EXPERTS].astype(i32)
    loff = info[:, 2, :N_EXPERTS].astype(i32)
    rows_e = total[0, :N_EXPERTS].astype(i32)
    region = ((rows_e + tm - 1) // tm) * tm
    ends = jnp.cumsum(region)
    tails = jnp.where(region > 0, ends - tm, -1).astype(i32)
    gbase = (ends - region)[None, :] + before
    n_pieces = (jnp.sum(run, axis=1) // ROW_GRAIN).astype(i32)
    piece_row = jnp.arange(LANES, dtype=i32) * ROW_GRAIN
    owner = jnp.sum(piece_row[None, :, None] >= (loff + run)[:, None, :], axis=-1)
    owner = jnp.minimum(owner, N_EXPERTS - 1)
    shift = gbase - loff
    pick = owner[..., None] == jnp.arange(N_EXPERTS, dtype=i32)
    dst = (jnp.sum(jnp.where(pick, shift[:, None, :], 0), axis=-1) + piece_row[None, :])
    dst = jnp.clip(dst, 0, n_tiles * tm - ROW_GRAIN).astype(i32).reshape(n_tok_tiles * LANES)
    n_used = (ends[-1] // tm).astype(i32).reshape(1)
    tile_start = jnp.arange(n_tiles, dtype=i32) * tm
    tile_start = jnp.minimum(tile_start, ends[-1] - tm)
    tile_expert = jnp.sum(tile_start[:, None] >= ends[None, :], axis=1).astype(i32)

    meta2 = meta.reshape(n_tok, LANES)
    xs = _dispatch(tails, n_pieces, dst, meta2, h2.reshape(n_tok, dm), n_tiles * tm)
    ys = _experts(tile_expert, n_used, xs, w_exp_gate[0], w_exp_up[0], w_exp_down[0], n_tiles)
    return _combine(n_pieces, dst, x1, meta, modall, norm_f_g.reshape(1, dm), ys)
```

```python
import functools

import jax
import jax.numpy as jnp
from jax import lax
from jax.experimental import pallas as pl
from jax.experimental.pallas import tpu as pltpu

F32 = jnp.float32
BF16 = jnp.bfloat16

D_MODEL = 1024
GLA_HEADS, GLA_DK, GLA_DV, GLA_RANK, GLA_TAU = 4, 64, 128, 16, 16.0
HG_HEADS, HG_DK, HG_DV = 4, 128, 128
GLA_QK = GLA_HEADS * GLA_DK
GLA_W = GLA_HEADS * GLA_DV
HG_W = HG_HEADS * HG_DK
CHUNK = 64
N_GROUPS, EXPERTS_PER_GROUP, N_EXPERTS, EXPERT_FF = 4, 8, 32, 256
EPS = 1e-6
LANES = 128

C_GQ, C_GK, C_GV, C_GG, C_GA = 0, 256, 512, 1024, 1536
C_HQ, C_HF, C_HI, C_HG, C_END = 1664, 2176, 2688, 3200, 3712

TOK_TILE = 256
MIX_TILES_PER_STEP = 2
EXP_TILE = 512
ROW_GRAIN = 8
LOCAL_ROWS = 2 * TOK_TILE + N_EXPERTS * ROW_GRAIN
MAX_PIECES = LOCAL_ROWS // ROW_GRAIN
EXP_CLAMP = 80.0
VMEM_LIMIT = 56 * 1024 * 1024


def _silu(t):
    return t * jax.nn.sigmoid(t)


def _dot(a, b):
    return jnp.dot(a, b, preferred_element_type=F32)


def _dot_nt(a, b):
    return lax.dot_general(a, b, (((1,), (1,)), ((), ())), preferred_element_type=F32)


def _dot_tn(a, b):
    return lax.dot_general(a, b, (((0,), (0,)), ((), ())), preferred_element_type=F32)


def _ada_kernel(c_ref, w_ref, b_ref, o_ref):
    ca = _silu(c_ref[...]).astype(BF16)
    o_ref[...] = _dot(ca, w_ref[...].astype(BF16)) + b_ref[...]


def _ada(c, w, b, tn):
    bsz, dm = c.shape
    n = w.shape[1]
    return pl.pallas_call(
        _ada_kernel,
        grid=(n // tn,),
        in_specs=[pl.BlockSpec((bsz, dm), lambda j: (0, 0)),
                  pl.BlockSpec((dm, tn), lambda j: (0, j)),
                  pl.BlockSpec((1, tn), lambda j: (0, j))],
        out_specs=pl.BlockSpec((bsz, tn), lambda j: (0, j)),
        out_shape=jax.ShapeDtypeStruct((bsz, n), F32),
        compiler_params=pltpu.CompilerParams(vmem_limit_bytes=VMEM_LIMIT),
        name="ada",
    )(c, w, b.reshape(1, n))


def _split2(t):
    hi = t.astype(BF16)
    lo = (t - hi.astype(F32)).astype(BF16)
    return hi, lo


def _recurrence(q, k, v, ld, st_ref, blocktril, causal, heads, dk, dv):
    t, w = q.shape
    nc = t // CHUNK
    ld_hi, ld_lo = _split2(ld)
    b = (_dot(blocktril, ld_hi) + _dot(blocktril, ld_lo)).reshape(nc, CHUNK, w)
    bm = b[:, CHUNK // 2 - 1:CHUNK // 2, :]
    bl = b[:, CHUNK - 1:CHUNK, :]
    qt = q.reshape(nc, CHUNK, w) * jnp.exp(jnp.minimum(b - bm, EXP_CLAMP))
    kt = k.reshape(nc, CHUNK, w) * jnp.exp(jnp.minimum(bm - b, EXP_CLAMP))
    qs = (qt * jnp.exp(bm)).astype(BF16).reshape(t, w)
    ks = (kt * jnp.exp(bl - bm)).astype(BF16).reshape(t, w)
    qt = qt.astype(BF16).reshape(t, w)
    kt = kt.astype(BF16).reshape(t, w)
    dec = jnp.exp(bl)
    vb = v.astype(BF16)
    states = [st_ref[h] for h in range(heads)]
    out_rows = []
    for c in range(nc):
        rs = slice(c * CHUNK, (c + 1) * CHUNK)
        outs = []
        for h in range(heads):
            ks_ = slice(h * dk, (h + 1) * dk)
            vs_ = slice(h * dv, (h + 1) * dv)
            attn = _dot_nt(qt[rs, ks_], kt[rs, ks_])
            attn = jnp.where(causal, attn, 0.0).astype(BF16)
            outs.append(_dot(attn, vb[rs, vs_]) + _dot_nt(qs[rs, ks_], states[h].astype(BF16)))
            states[h] = states[h] * dec[c][:, ks_] + _dot_tn(vb[rs, vs_], ks[rs, ks_])
        out_rows.append(jnp.concatenate(outs, axis=-1))
    for h in range(heads):
        st_ref[h] = states[h]
    return jnp.concatenate(out_rows, axis=0)


def _head_norm(o, g, heads, dv):
    outs = []
    for h in range(heads):
        oh = o[:, h * dv:(h + 1) * dv]
        r = lax.rsqrt(jnp.mean(oh * oh, axis=-1, keepdims=True) + EPS)
        outs.append(oh * r * g)
    return jnp.concatenate(outs, axis=-1)


def _mixer_kernel(x_ref, mod_ref, n1g_ref, win_ref, wa2_ref, ba_ref, glag_ref, hgg_ref, lbl_ref,
                  wout_ref, n2g_ref, wr_ref, br_ref,
                  x1_ref, h2_ref, meta_ref, info_ref, cnt_ref,
                  proj_ref, o_ref, sg_ref, sh_ref, carry_ref):
    bi = pl.program_id(0)
    si = pl.program_id(1)

    @pl.when(si == 0)
    def _():
        sg_ref[...] = jnp.zeros_like(sg_ref)
        sh_ref[...] = jnp.zeros_like(sh_ref)

    @pl.when((bi == 0) & (si == 0))
    def _():
        carry_ref[...] = jnp.zeros_like(carry_ref)

    for ti in range(x_ref.shape[1] // TOK_TILE):
        rows = slice(ti * TOK_TILE, (ti + 1) * TOK_TILE)
        _mixer_tile(x_ref.at[0, rows, :], mod_ref, n1g_ref, win_ref, wa2_ref, ba_ref, glag_ref, hgg_ref,
                    lbl_ref, wout_ref, n2g_ref, wr_ref, br_ref,
                    x1_ref.at[0, rows, :], h2_ref.at[0, rows, :], meta_ref.at[0, rows, :],
                    info_ref.at[ti], cnt_ref, proj_ref.at[ti], o_ref.at[ti], sg_ref, sh_ref, carry_ref)


def _mixer_tile(x_ref, mod_ref, n1g_ref, win_ref, wa2_ref, ba_ref, glag_ref, hgg_ref, lbl_ref,
                wout_ref, n2g_ref, wr_ref, br_ref,
                x1_ref, h2_ref, meta_ref, info_ref, cnt_ref,
                proj_ref, o_ref, sg_ref, sh_ref, carry_ref):
    tile = TOK_TILE
    mod = mod_ref[0]
    sh1, sc1, gt1 = mod[0:1], mod[1:2], mod[2:3]
    sh2, sc2 = mod[3:4], mod[4:5]

    x = x_ref[...]
    r1 = lax.rsqrt(jnp.mean(x * x, axis=-1, keepdims=True) + EPS)
    h = (x * r1 * n1g_ref[...]) * (1.0 + sc1) + sh1
    proj_ref[...] = _dot(h.astype(BF16), win_ref[...])

    a_logit = _dot(proj_ref[:, C_GA:C_GA + LANES].astype(BF16), wa2_ref[...]) + ba_ref[...]
    log_sig = jnp.minimum(a_logit, 0.0) - jnp.log(1.0 + jnp.exp(-jnp.abs(a_logit)))
    ld_g = log_sig * (1.0 / GLA_TAU)

    lbl = lbl_ref[...]
    lmax = jnp.max(lbl, axis=0, keepdims=True)
    lexp = jnp.exp(lbl - lmax)
    lb = lexp[0:1] / jnp.sum(lexp, axis=0, keepdims=True)

    row = lax.broadcasted_iota(jnp.int32, (CHUNK, CHUNK), 0)
    col = lax.broadcasted_iota(jnp.int32, (CHUNK, CHUNK), 1)
    causal = row >= col
    trow = lax.broadcasted_iota(jnp.int32, (tile, tile), 0)
    tcol = lax.broadcasted_iota(jnp.int32, (tile, tile), 1)
    same_chunk = (trow - tcol) <= (trow & (CHUNK - 1))
    blocktril = ((trow >= tcol) & same_chunk).astype(BF16)

    q = proj_ref[:, C_GQ:C_GK] * (GLA_DK ** -0.5)
    o_a = _recurrence(q, proj_ref[:, C_GK:C_GV], proj_ref[:, C_GV:C_GG], ld_g, sg_ref,
                      blocktril, causal, GLA_HEADS, GLA_DK, GLA_DV)
    o_a = _head_norm(o_a, glag_ref[...], GLA_HEADS, GLA_DV) * _silu(proj_ref[:, C_GG:C_GA])
    o_ref[:, 0:GLA_W] = o_a.astype(BF16)
    hq = proj_ref[:, C_HQ:C_HF] * (HG_DK ** -0.5)
    sig = jax.nn.sigmoid(proj_ref[:, C_HF:C_HI])
    f = lb + (1.0 - lb) * sig
    kh = (1.0 - lb) * (1.0 - sig)
    o_b = _recurrence(hq, kh, proj_ref[:, C_HI:C_HG], jnp.log(f), sh_ref,
                      blocktril, causal, HG_HEADS, HG_DK, HG_DV)
    o_b = _head_norm(o_b, hgg_ref[...], HG_HEADS, HG_DV) * _silu(proj_ref[:, C_HG:C_END])
    o_ref[:, GLA_W:GLA_W + HG_W] = o_b.astype(BF16)

    mix = _dot(o_ref[...], wout_ref[...])
    x1 = x_ref[...] + gt1 * mix
    x1_ref[...] = x1
    r2 = lax.rsqrt(jnp.mean(x1 * x1, axis=-1, keepdims=True) + EPS)
    h2 = (x1 * r2 * n2g_ref[...]) * (1.0 + sc2) + sh2
    h2b = h2.astype(BF16)
    h2_ref[...] = h2b

    logits = _dot(h2b, wr_ref[...]) + br_ref[...]
    lane = lax.broadcasted_iota(jnp.int32, (tile, LANES), 1)
    neg = jnp.float32(-jnp.inf)
    big = jnp.int32(1 << 20)
    gl = jnp.where((lane >= N_EXPERTS) & (lane < N_EXPERTS + N_GROUPS), logits, neg)
    gmax = jnp.max(gl, axis=-1, keepdims=True)
    gidx = jnp.min(jnp.where(gl == gmax, lane, big), axis=-1, keepdims=True) - N_EXPERTS
    g_p = 1.0 / jnp.sum(jnp.exp(gl - gmax), axis=-1, keepdims=True)
    in_group = (lane >= gidx * EXPERTS_PER_GROUP) & (lane < (gidx + 1) * EXPERTS_PER_GROUP)
    el = jnp.where(in_group, logits, neg)
    m1 = jnp.max(el, axis=-1, keepdims=True)
    i1 = jnp.min(jnp.where(el == m1, lane, big), axis=-1, keepdims=True)
    el2 = jnp.where(lane == i1, neg, el)
    m2 = jnp.max(el2, axis=-1, keepdims=True)
    i2 = jnp.min(jnp.where(el2 == m2, lane, big), axis=-1, keepdims=True)
    t = jnp.exp(m2 - m1)
    w1 = g_p / (1.0 + t)
    w2 = g_p * t / (1.0 + t)

    sel1 = lane == i1
    sel2 = lane == i2
    onehot = (sel1 | sel2).astype(BF16)
    trow = lax.broadcasted_iota(jnp.int32, (tile, tile), 0)
    tcol = lax.broadcasted_iota(jnp.int32, (tile, tile), 1)
    strict = (trow > tcol).astype(BF16)
    lcum = _dot(strict, onehot)
    cnt = jnp.sum(onehot.astype(F32), axis=0, keepdims=True)
    run = jnp.floor((cnt + (ROW_GRAIN - 1)) * (1.0 / ROW_GRAIN)) * ROW_GRAIN
    erow = lax.broadcasted_iota(jnp.int32, (LANES, LANES), 0)
    ecol = lax.broadcasted_iota(jnp.int32, (LANES, LANES), 1)
    before = (erow < ecol).astype(BF16)
    run8 = jnp.broadcast_to(run, (8, LANES))
    loff = _dot(run8.astype(BF16), before)
    pos = lcum + loff[0:1]
    p1 = jnp.sum(jnp.where(sel1, pos, 0.0), axis=-1, keepdims=True)
    p2 = jnp.sum(jnp.where(sel2, pos, 0.0), axis=-1, keepdims=True)

    srow = lax.broadcasted_iota(jnp.int32, (8, LANES), 0)
    info = jnp.where(srow == 0, run8, jnp.where(srow == 1, carry_ref[...], jnp.where(srow == 2, loff, 0.0)))
    info_ref[...] = info
    carry_ref[...] = carry_ref[...] + run
    cnt_ref[...] = carry_ref[...]

    meta = jnp.where(lane == 0, p1, 0.0)
    meta = jnp.where(lane == 1, p2, meta)
    meta = jnp.where(lane == 2, w1, meta)
    meta = jnp.where(lane == 3, w2, meta)
    meta_ref[...] = meta


def _mixer(x, modall, n1g, win, wa2, ba, glag, hgg, lbl, wout, n2g, wr, br):
    bsz, seq, dm = x.shape
    tps = MIX_TILES_PER_STEP
    tile = tps * TOK_TILE
    spt = seq // tile
    grid = (bsz, spt)
    const = lambda shape: pl.BlockSpec(shape, lambda b, s: (0,) * len(shape))
    tok = lambda width: pl.BlockSpec((1, tile, width), lambda b, s: (b, s, 0))
    return pl.pallas_call(
        _mixer_kernel,
        grid=grid,
        in_specs=[tok(dm),
                  pl.BlockSpec((1, 8, dm), lambda b, s: (b, 0, 0)),
                  const((1, dm)), const((dm, C_END)), const((LANES, GLA_QK)), const((1, GLA_QK)),
                  const((1, GLA_DV)), const((1, HG_DV)), const((2, HG_W)),
                  const((dm, dm)), const((1, dm)), const((dm, LANES)), const((1, LANES))],
        out_specs=[tok(dm), tok(dm), tok(LANES),
                   pl.BlockSpec((tps, 8, LANES), lambda b, s: (b * spt + s, 0, 0)),
                   const((1, LANES))],
        out_shape=[jax.ShapeDtypeStruct((bsz, seq, dm), F32),
                   jax.ShapeDtypeStruct((bsz, seq, dm), BF16),
                   jax.ShapeDtypeStruct((bsz, seq, LANES), F32),
                   jax.ShapeDtypeStruct((bsz * spt * tps, 8, LANES), F32),
                   jax.ShapeDtypeStruct((1, LANES), F32)],
        scratch_shapes=[pltpu.VMEM((tps, TOK_TILE, C_END), F32),
                        pltpu.VMEM((tps, TOK_TILE, dm), BF16),
                        pltpu.VMEM((GLA_HEADS, GLA_DV, GLA_DK), F32),
                        pltpu.VMEM((HG_HEADS, HG_DV, HG_DK), F32),
                        pltpu.VMEM((1, LANES), F32)],
        compiler_params=pltpu.CompilerParams(
            dimension_semantics=("arbitrary", "arbitrary"), vmem_limit_bytes=VMEM_LIMIT),
        name="mixer",
    )(x, modall, n1g, win, wa2, ba, glag, hgg, lbl, wout, n2g, wr, br)


def _local_positions(meta, rows):
    lane = lax.broadcasted_iota(jnp.int32, (meta.shape[0], rows), 1)
    p1 = meta[:, 0:1].astype(jnp.int32)
    p2 = meta[:, 1:2].astype(jnp.int32)
    return lane == p1, lane == p2


def _dispatch_kernel(tails_ref, np_ref, dst_ref, meta_ref, h2_ref, xs_ref, xl_ref, zero_ref, sem, zsem):
    @pl.when(pl.program_id(0) == 0)
    def _():
        zero_ref[...] = jnp.zeros_like(zero_ref)

        def zero_copy(e):
            return pltpu.make_async_copy(
                zero_ref, xs_ref.at[pl.ds(pl.multiple_of(tails_ref[e], 8), EXP_TILE), :], zsem)

        for e in range(N_EXPERTS):
            @pl.when(tails_ref[e] >= 0)
            def _():
                zero_copy(e).start()
        for e in range(N_EXPERTS):
            @pl.when(tails_ref[e] >= 0)
            def _():
                zero_copy(e).wait()

    step = pl.program_id(0)
    slot = step % 2

    sel1, sel2 = _local_positions(meta_ref[...], LOCAL_ROWS)
    xl_ref[slot] = _dot_tn((sel1 | sel2).astype(BF16), h2_ref[...])

    def piece(j, buf, dst_row):
        src = xl_ref.at[buf, pl.ds(pl.multiple_of(j * ROW_GRAIN, ROW_GRAIN), ROW_GRAIN), :]
        dst = xs_ref.at[pl.ds(pl.multiple_of(dst_row, ROW_GRAIN), ROW_GRAIN), :]
        return pltpu.make_async_copy(src, dst, sem.at[buf])

    def issue(j, carry):
        piece(j, slot, dst_ref[j]).start()
        return carry

    lax.fori_loop(0, np_ref[step], issue, 0)

    def drain(buf):
        def body(j, carry):
            piece(j, buf, 0).wait()
            return carry
        return body

    @pl.when(step > 0)
    def _():
        lax.fori_loop(0, np_ref[step - 1], drain(1 - slot), 0)

    @pl.when(step == pl.num_programs(0) - 1)
    def _():
        lax.fori_loop(0, np_ref[step], drain(slot), 0)


def _dispatch(tails, n_pieces, dst, meta, h2, n_rows):
    n, dm = h2.shape
    tile = TOK_TILE
    return pl.pallas_call(
        _dispatch_kernel,
        grid_spec=pltpu.PrefetchScalarGridSpec(
            num_scalar_prefetch=2,
            grid=(n // tile,),
            in_specs=[pl.BlockSpec((LANES,), lambda i, tl, npc: (i,), memory_space=pltpu.SMEM),
                      pl.BlockSpec((tile, LANES), lambda i, tl, npc: (i, 0)),
                      pl.BlockSpec((tile, dm), lambda i, tl, npc: (i, 0))],
            out_specs=pl.BlockSpec(memory_space=pl.ANY),
            scratch_shapes=[pltpu.VMEM((2, LOCAL_ROWS, dm), F32),
                            pltpu.VMEM((EXP_TILE, dm), F32),
                            pltpu.SemaphoreType.DMA((2,)), pltpu.SemaphoreType.DMA(())]),
        out_shape=jax.ShapeDtypeStruct((n_rows, dm), F32),
        compiler_params=pltpu.CompilerParams(
            dimension_semantics=("arbitrary",), vmem_limit_bytes=VMEM_LIMIT),
        name="dispatch",
    )(tails, n_pieces, dst, meta, h2)


def _experts_kernel(te_ref, nu_ref, xs_ref, wg_ref, wu_ref, wd_ref, ys_ref):
    i = pl.program_id(0)

    @pl.when(i < nu_ref[0])
    def _():
        xb = xs_ref[...].astype(BF16)
        g = _dot(xb, wg_ref[0].astype(BF16))
        u = _dot(xb, wu_ref[0].astype(BF16))
        a = (_silu(g) * u).astype(BF16)
        ys_ref[...] = _dot(a, wd_ref[0].astype(BF16))


def _experts(tile_expert, n_used, xs, wg, wu, wd, n_tiles):
    rows, dm = xs.shape
    tm = EXP_TILE
    ff = wg.shape[-1]

    def row_map(i, te, nu):
        return (jnp.minimum(i, nu[0] - 1), 0)

    def out_map(i, te, nu):
        return (jnp.where(i < nu[0], i, n_tiles), 0)

    return pl.pallas_call(
        _experts_kernel,
        grid_spec=pltpu.PrefetchScalarGridSpec(
            num_scalar_prefetch=2,
            grid=(n_tiles,),
            in_specs=[pl.BlockSpec((tm, dm), row_map),
                      pl.BlockSpec((1, dm, ff), lambda i, te, nu: (te[i], 0, 0)),
                      pl.BlockSpec((1, dm, ff), lambda i, te, nu: (te[i], 0, 0)),
                      pl.BlockSpec((1, ff, dm), lambda i, te, nu: (te[i], 0, 0))],
            out_specs=pl.BlockSpec((tm, dm), out_map)),
        out_shape=jax.ShapeDtypeStruct(((n_tiles + 1) * tm, dm), F32),
        compiler_params=pltpu.CompilerParams(
            dimension_semantics=("arbitrary",), vmem_limit_bytes=VMEM_LIMIT),
        name="experts",
    )(tile_expert, n_used, xs, wg, wu, wd)


def _combine_kernel(np_ref, src_ref, src_next_ref, x1_ref, meta_ref, mod_ref, nfg_ref, ys_ref, out_ref,
                    yl_ref, sem):
    step = pl.program_id(0) * pl.num_programs(1) + pl.program_id(1)
    last = pl.num_programs(0) * pl.num_programs(1) - 1
    slot = step % 2

    def piece(j, buf, src_row):
        src = ys_ref.at[pl.ds(pl.multiple_of(src_row, ROW_GRAIN), ROW_GRAIN), :]
        dst = yl_ref.at[buf, pl.ds(pl.multiple_of(j * ROW_GRAIN, ROW_GRAIN), ROW_GRAIN), :]
        return pltpu.make_async_copy(src, dst, sem.at[buf])

    @pl.when(step == 0)
    def _():
        yl_ref[...] = jnp.zeros_like(yl_ref)

        def first(j, carry):
            piece(j, 0, src_ref[j]).start()
            return carry

        lax.fori_loop(0, np_ref[0], first, 0)

    @pl.when(step < last)
    def _():
        def ahead(j, carry):
            piece(j, 1 - slot, src_next_ref[j]).start()
            return carry

        lax.fori_loop(0, np_ref[step + 1], ahead, 0)

    def drain(j, carry):
        piece(j, slot, 0).wait()
        return carry

    lax.fori_loop(0, np_ref[step], drain, 0)

    mod = mod_ref[0]
    gt2, shf, scf = mod[5:6], mod[6:7], mod[7:8]
    meta = meta_ref[0]
    sel1, sel2 = _local_positions(meta, LOCAL_ROWS)
    wsel = jnp.where(sel1, meta[:, 2:3], 0.0) + jnp.where(sel2, meta[:, 3:4], 0.0)
    ffn = _dot(wsel.astype(BF16), yl_ref[slot].astype(BF16))
    x2 = x1_ref[0] + gt2 * ffn
    r = lax.rsqrt(jnp.mean(x2 * x2, axis=-1, keepdims=True) + EPS)
    out_ref[0] = (x2 * r * nfg_ref[...]) * (1.0 + scf) + shf


def _combine(n_pieces, dst, x1, meta, modall, nfg, ys):
    bsz, seq, dm = x1.shape
    tile = TOK_TILE
    spt = seq // tile
    return pl.pallas_call(
        _combine_kernel,
        grid_spec=pltpu.PrefetchScalarGridSpec(
            num_scalar_prefetch=1,
            grid=(bsz, spt),
            in_specs=[pl.BlockSpec((LANES,), lambda b, s, npc: (b * spt + s,), memory_space=pltpu.SMEM),
                      pl.BlockSpec((LANES,), lambda b, s, npc: (jnp.minimum(b * spt + s + 1, bsz * spt - 1),),
                                   memory_space=pltpu.SMEM),
                      pl.BlockSpec((1, tile, dm), lambda b, s, npc: (b, s, 0)),
                      pl.BlockSpec((1, tile, LANES), lambda b, s, npc: (b, s, 0)),
                      pl.BlockSpec((1, 8, dm), lambda b, s, npc: (b, 0, 0)),
                      pl.BlockSpec((1, dm), lambda b, s, npc: (0, 0)),
                      pl.BlockSpec(memory_space=pl.ANY)],
            out_specs=pl.BlockSpec((1, tile, dm), lambda b, s, npc: (b, s, 0)),
            scratch_shapes=[pltpu.VMEM((2, LOCAL_ROWS, dm), F32), pltpu.SemaphoreType.DMA((2,))]),
        out_shape=jax.ShapeDtypeStruct((bsz, seq, dm), F32),
        compiler_params=pltpu.CompilerParams(
            dimension_semantics=("arbitrary", "arbitrary"), vmem_limit_bytes=VMEM_LIMIT),
        name="combine",
    )(n_pieces, dst, dst, x1, meta, modall, nfg, ys)


def kernel(x, c, w_ada, b_ada, norm1_g, w_in, gla_w_a2, gla_b_a, gla_norm_g, hg_norm_g, hg_lb_logits,
           w_out, norm2_g, w_rg, b_rg, w_re, b_re, w_exp_gate, w_exp_up, w_exp_down, w_ada_final,
           b_ada_final, norm_f_g):
    bsz, seq, dm = x.shape
    n_tok = bsz * seq

    mod = _ada(c, w_ada[0], b_ada[0], 1536)
    mod_f = _ada(c, w_ada_final, b_ada_final, 1024)
    modall = jnp.concatenate([mod.reshape(bsz, 6, dm), mod_f.reshape(bsz, 2, dm)], axis=1)

    wi = w_in[0]
    ga0 = 2 * GLA_QK + 2 * GLA_W
    win = jnp.concatenate([wi[:, :ga0],
                           jnp.pad(wi[:, ga0:ga0 + GLA_RANK], ((0, 0), (0, LANES - GLA_RANK))),
                           wi[:, ga0 + GLA_RANK:]], axis=1).astype(BF16)
    wa2 = jnp.pad(gla_w_a2[0], ((0, LANES - GLA_RANK), (0, 0))).astype(BF16)
    pad_r = LANES - N_EXPERTS - N_GROUPS
    wr = jnp.pad(jnp.concatenate([w_re[0], w_rg[0]], axis=1), ((0, 0), (0, pad_r))).astype(BF16)
    br = jnp.pad(jnp.concatenate([b_re[0], b_rg[0]]), (0, pad_r)).reshape(1, LANES)

    assert seq % (TOK_TILE * MIX_TILES_PER_STEP) == 0
    x1, h2, meta, info, total = _mixer(
        x, modall, norm1_g[0].reshape(1, dm), win, wa2, gla_b_a[0].reshape(1, GLA_QK),
        gla_norm_g[0].reshape(1, GLA_DV), hg_norm_g[0].reshape(1, HG_DV), hg_lb_logits,
        w_out[0].astype(BF16), norm2_g[0].reshape(1, dm), wr, br)

    tm = EXP_TILE
    n_tok_tiles = n_tok // TOK_TILE
    n_tiles = (2 * n_tok + n_tok_tiles * N_EXPERTS * (ROW_GRAIN - 1)) // tm + N_EXPERTS
    i32 = jnp.int32
    run = info[:, 0, :N_EXPERTS].astype(i32)
    before = info[:, 1, :N_EXPERTS].astype(i32)
    loff = info[:, 2, :N_EXPERTS].astype(i32)
    rows_e = total[0, :N_EXPERTS].astype(i32)
    region = ((rows_e + tm - 1) // tm) * tm
    ends = jnp.cumsum(region)
    tails = jnp.where(region > 0, ends - tm, -1).astype(i32)
    gbase = (ends - region)[None, :] + before
    n_pieces = (jnp.sum(run, axis=1) // ROW_GRAIN).astype(i32)
    piece_row = jnp.arange(LANES, dtype=i32) * ROW_GRAIN
    owner = jnp.sum(piece_row[None, :, None] >= (loff + run)[:, None, :], axis=-1)
    owner = jnp.minimum(owner, N_EXPERTS - 1)
    shift = gbase - loff
    pick = owner[..., None] == jnp.arange(N_EXPERTS, dtype=i32)
    dst = (jnp.sum(jnp.where(pick, shift[:, None, :], 0), axis=-1) + piece_row[None, :])
    dst = jnp.clip(dst, 0, n_tiles * tm - ROW_GRAIN).astype(i32).reshape(n_tok_tiles * LANES)
    n_used = (ends[-1] // tm).astype(i32).reshape(1)
    tile_start = jnp.arange(n_tiles, dtype=i32) * tm
    tile_start = jnp.minimum(tile_start, ends[-1] - tm)
    tile_expert = jnp.sum(tile_start[:, None] >= ends[None, :], axis=1).astype(i32)

    meta2 = meta.reshape(n_tok, LANES)
    xs = _dispatch(tails, n_pieces, dst, meta2, h2.reshape(n_tok, dm), n_tiles * tm)
    ys = _experts(tile_expert, n_used, xs, w_exp_gate[0], w_exp_up[0], w_exp_down[0], n_tiles)
    return _combine(n_pieces, dst, x1, meta, modall, norm_f_g.reshape(1, dm), ys)
```

```python
import functools

import jax
import jax.numpy as jnp
from jax import lax
from jax.experimental import pallas as pl
from jax.experimental.pallas import tpu as pltpu

F32 = jnp.float32
BF16 = jnp.bfloat16

D_MODEL = 1024
GLA_HEADS, GLA_DK, GLA_DV, GLA_RANK, GLA_TAU = 4, 64, 128, 16, 16.0
HG_HEADS, HG_DK, HG_DV = 4, 128, 128
GLA_QK = GLA_HEADS * GLA_DK
GLA_W = GLA_HEADS * GLA_DV
HG_W = HG_HEADS * HG_DK
CHUNK = 64
N_GROUPS, EXPERTS_PER_GROUP, N_EXPERTS, EXPERT_FF = 4, 8, 32, 256
EPS = 1e-6
LANES = 128

C_GQ, C_GK, C_GV, C_GG, C_GA = 0, 256, 512, 1024, 1536
C_HQ, C_HF, C_HI, C_HG, C_END = 1664, 2176, 2688, 3200, 3712

TOK_TILE = 256
MIX_TILES_PER_STEP = 2
EXP_TILE = 512
ROW_GRAIN = 16
LOCAL_ROWS = 2 * TOK_TILE + N_EXPERTS * ROW_GRAIN
MAX_PIECES = LOCAL_ROWS // ROW_GRAIN
EXP_CLAMP = 80.0
VMEM_LIMIT = 56 * 1024 * 1024


def _silu(t):
    return t * jax.nn.sigmoid(t)


def _dot(a, b):
    return jnp.dot(a, b, preferred_element_type=F32)


def _dot_nt(a, b):
    return lax.dot_general(a, b, (((1,), (1,)), ((), ())), preferred_element_type=F32)


def _dot_tn(a, b):
    return lax.dot_general(a, b, (((0,), (0,)), ((), ())), preferred_element_type=F32)


def _ada_kernel(c_ref, w_ref, b_ref, o_ref):
    ca = _silu(c_ref[...]).astype(BF16)
    o_ref[...] = _dot(ca, w_ref[...].astype(BF16)) + b_ref[...]


def _ada(c, w, b, tn):
    bsz, dm = c.shape
    n = w.shape[1]
    return pl.pallas_call(
        _ada_kernel,
        grid=(n // tn,),
        in_specs=[pl.BlockSpec((bsz, dm), lambda j: (0, 0)),
                  pl.BlockSpec((dm, tn), lambda j: (0, j)),
                  pl.BlockSpec((1, tn), lambda j: (0, j))],
        out_specs=pl.BlockSpec((bsz, tn), lambda j: (0, j)),
        out_shape=jax.ShapeDtypeStruct((bsz, n), F32),
        compiler_params=pltpu.CompilerParams(vmem_limit_bytes=VMEM_LIMIT),
        name="ada",
    )(c, w, b.reshape(1, n))


def _split2(t):
    hi = t.astype(BF16)
    lo = (t - hi.astype(F32)).astype(BF16)
    return hi, lo


def _recurrence(q, k, v, ld, st_ref, blocktril, causal, heads, dk, dv):
    t, w = q.shape
    nc = t // CHUNK
    ld_hi, ld_lo = _split2(ld)
    b = (_dot(blocktril, ld_hi) + _dot(blocktril, ld_lo)).reshape(nc, CHUNK, w)
    bm = b[:, CHUNK // 2 - 1:CHUNK // 2, :]
    bl = b[:, CHUNK - 1:CHUNK, :]
    qt = q.reshape(nc, CHUNK, w) * jnp.exp(jnp.minimum(b - bm, EXP_CLAMP))
    kt = k.reshape(nc, CHUNK, w) * jnp.exp(jnp.minimum(bm - b, EXP_CLAMP))
    qs = (qt * jnp.exp(bm)).astype(BF16).reshape(t, w)
    ks = (kt * jnp.exp(bl - bm)).astype(BF16).reshape(t, w)
    qt = qt.astype(BF16).reshape(t, w)
    kt = kt.astype(BF16).reshape(t, w)
    dec = jnp.exp(bl)
    vb = v.astype(BF16)
    states = [st_ref[h] for h in range(heads)]
    out_rows = []
    for c in range(nc):
        rs = slice(c * CHUNK, (c + 1) * CHUNK)
        outs = []
        for h in range(heads):
            ks_ = slice(h * dk, (h + 1) * dk)
            vs_ = slice(h * dv, (h + 1) * dv)
            attn = _dot_nt(qt[rs, ks_], kt[rs, ks_])
            attn = jnp.where(causal, attn, 0.0).astype(BF16)
            outs.append(_dot(attn, vb[rs, vs_]) + _dot_nt(qs[rs, ks_], states[h].astype(BF16)))
            states[h] = states[h] * dec[c][:, ks_] + _dot_tn(vb[rs, vs_], ks[rs, ks_])
        out_rows.append(jnp.concatenate(outs, axis=-1))
    for h in range(heads):
        st_ref[h] = states[h]
    return jnp.concatenate(out_rows, axis=0)


def _head_norm(o, g, heads, dv):
    outs = []
    for h in range(heads):
        oh = o[:, h * dv:(h + 1) * dv]
        r = lax.rsqrt(jnp.mean(oh * oh, axis=-1, keepdims=True) + EPS)
        outs.append(oh * r * g)
    return jnp.concatenate(outs, axis=-1)


def _mixer_kernel(x_ref, mod_ref, n1g_ref, win_ref, wa2_ref, ba_ref, glag_ref, hgg_ref, lbl_ref,
                  wout_ref, n2g_ref, wr_ref, br_ref,
                  x1_ref, h2_ref, meta_ref, info_ref, cnt_ref,
                  proj_ref, o_ref, sg_ref, sh_ref, carry_ref):
    bi = pl.program_id(0)
    si = pl.program_id(1)

    @pl.when(si == 0)
    def _():
        sg_ref[...] = jnp.zeros_like(sg_ref)
        sh_ref[...] = jnp.zeros_like(sh_ref)

    @pl.when((bi == 0) & (si == 0))
    def _():
        carry_ref[...] = jnp.zeros_like(carry_ref)

    for ti in range(x_ref.shape[1] // TOK_TILE):
        rows = slice(ti * TOK_TILE, (ti + 1) * TOK_TILE)
        _mixer_tile(x_ref.at[0, rows, :], mod_ref, n1g_ref, win_ref, wa2_ref, ba_ref, glag_ref, hgg_ref,
                    lbl_ref, wout_ref, n2g_ref, wr_ref, br_ref,
                    x1_ref.at[0, rows, :], h2_ref.at[0, rows, :], meta_ref.at[0, rows, :],
                    info_ref.at[ti], cnt_ref, proj_ref.at[ti], o_ref.at[ti], sg_ref, sh_ref, carry_ref)


def _mixer_tile(x_ref, mod_ref, n1g_ref, win_ref, wa2_ref, ba_ref, glag_ref, hgg_ref, lbl_ref,
                wout_ref, n2g_ref, wr_ref, br_ref,
                x1_ref, h2_ref, meta_ref, info_ref, cnt_ref,
                proj_ref, o_ref, sg_ref, sh_ref, carry_ref):
    tile = TOK_TILE
    mod = mod_ref[0]
    sh1, sc1, gt1 = mod[0:1], mod[1:2], mod[2:3]
    sh2, sc2 = mod[3:4], mod[4:5]

    x = x_ref[...]
    r1 = lax.rsqrt(jnp.mean(x * x, axis=-1, keepdims=True) + EPS)
    h = (x * r1 * n1g_ref[...]) * (1.0 + sc1) + sh1
    proj_ref[...] = _dot(h.astype(BF16), win_ref[...])

    a_logit = _dot(proj_ref[:, C_GA:C_GA + LANES].astype(BF16), wa2_ref[...]) + ba_ref[...]
    log_sig = jnp.minimum(a_logit, 0.0) - jnp.log(1.0 + jnp.exp(-jnp.abs(a_logit)))
    ld_g = log_sig * (1.0 / GLA_TAU)

    lbl = lbl_ref[...]
    lmax = jnp.max(lbl, axis=0, keepdims=True)
    lexp = jnp.exp(lbl - lmax)
    lb = lexp[0:1] / jnp.sum(lexp, axis=0, keepdims=True)

    row = lax.broadcasted_iota(jnp.int32, (CHUNK, CHUNK), 0)
    col = lax.broadcasted_iota(jnp.int32, (CHUNK, CHUNK), 1)
    causal = row >= col
    trow = lax.broadcasted_iota(jnp.int32, (tile, tile), 0)
    tcol = lax.broadcasted_iota(jnp.int32, (tile, tile), 1)
    same_chunk = (trow - tcol) <= (trow & (CHUNK - 1))
    blocktril = ((trow >= tcol) & same_chunk).astype(BF16)

    q = proj_ref[:, C_GQ:C_GK] * (GLA_DK ** -0.5)
    o_a = _recurrence(q, proj_ref[:, C_GK:C_GV], proj_ref[:, C_GV:C_GG], ld_g, sg_ref,
                      blocktril, causal, GLA_HEADS, GLA_DK, GLA_DV)
    o_a = _head_norm(o_a, glag_ref[...], GLA_HEADS, GLA_DV) * _silu(proj_ref[:, C_GG:C_GA])
    o_ref[:, 0:GLA_W] = o_a.astype(BF16)
    hq = proj_ref[:, C_HQ:C_HF] * (HG_DK ** -0.5)
    sig = jax.nn.sigmoid(proj_ref[:, C_HF:C_HI])
    f = lb + (1.0 - lb) * sig
    kh = (1.0 - lb) * (1.0 - sig)
    o_b = _recurrence(hq, kh, proj_ref[:, C_HI:C_HG], jnp.log(f), sh_ref,
                      blocktril, causal, HG_HEADS, HG_DK, HG_DV)
    o_b = _head_norm(o_b, hgg_ref[...], HG_HEADS, HG_DV) * _silu(proj_ref[:, C_HG:C_END])
    o_ref[:, GLA_W:GLA_W + HG_W] = o_b.astype(BF16)

    mix = _dot(o_ref[...], wout_ref[...])
    x1 = x_ref[...] + gt1 * mix
    x1_ref[...] = x1
    r2 = lax.rsqrt(jnp.mean(x1 * x1, axis=-1, keepdims=True) + EPS)
    h2 = (x1 * r2 * n2g_ref[...]) * (1.0 + sc2) + sh2
    h2b = h2.astype(BF16)
    h2_ref[...] = h2b

    logits = _dot(h2b, wr_ref[...]) + br_ref[...]
    lane = lax.broadcasted_iota(jnp.int32, (tile, LANES), 1)
    neg = jnp.float32(-jnp.inf)
    big = jnp.int32(1 << 20)
    gl = jnp.where((lane >= N_EXPERTS) & (lane < N_EXPERTS + N_GROUPS), logits, neg)
    gmax = jnp.max(gl, axis=-1, keepdims=True)
    gidx = jnp.min(jnp.where(gl == gmax, lane, big), axis=-1, keepdims=True) - N_EXPERTS
    g_p = 1.0 / jnp.sum(jnp.exp(gl - gmax), axis=-1, keepdims=True)
    in_group = (lane >= gidx * EXPERTS_PER_GROUP) & (lane < (gidx + 1) * EXPERTS_PER_GROUP)
    el = jnp.where(in_group, logits, neg)
    m1 = jnp.max(el, axis=-1, keepdims=True)
    i1 = jnp.min(jnp.where(el == m1, lane, big), axis=-1, keepdims=True)
    el2 = jnp.where(lane == i1, neg, el)
    m2 = jnp.max(el2, axis=-1, keepdims=True)
    i2 = jnp.min(jnp.where(el2 == m2, lane, big), axis=-1, keepdims=True)
    t = jnp.exp(m2 - m1)
    w1 = g_p / (1.0 + t)
    w2 = g_p * t / (1.0 + t)

    sel1 = lane == i1
    sel2 = lane == i2
    onehot = (sel1 | sel2).astype(BF16)
    trow = lax.broadcasted_iota(jnp.int32, (tile, tile), 0)
    tcol = lax.broadcasted_iota(jnp.int32, (tile, tile), 1)
    strict = (trow > tcol).astype(BF16)
    lcum = _dot(strict, onehot)
    cnt = jnp.sum(onehot.astype(F32), axis=0, keepdims=True)
    run = jnp.floor((cnt + (ROW_GRAIN - 1)) * (1.0 / ROW_GRAIN)) * ROW_GRAIN
    erow = lax.broadcasted_iota(jnp.int32, (LANES, LANES), 0)
    ecol = lax.broadcasted_iota(jnp.int32, (LANES, LANES), 1)
    before = (erow < ecol).astype(BF16)
    run8 = jnp.broadcast_to(run, (8, LANES))
    loff = _dot(run8.astype(BF16), before)
    pos = lcum + loff[0:1]
    p1 = jnp.sum(jnp.where(sel1, pos, 0.0), axis=-1, keepdims=True)
    p2 = jnp.sum(jnp.where(sel2, pos, 0.0), axis=-1, keepdims=True)

    srow = lax.broadcasted_iota(jnp.int32, (8, LANES), 0)
    info = jnp.where(srow == 0, run8, jnp.where(srow == 1, carry_ref[...], jnp.where(srow == 2, loff, 0.0)))
    info_ref[...] = info
    carry_ref[...] = carry_ref[...] + run
    cnt_ref[...] = carry_ref[...]

    meta = jnp.where(lane == 0, p1, 0.0)
    meta = jnp.where(lane == 1, p2, meta)
    meta = jnp.where(lane == 2, w1, meta)
    meta = jnp.where(lane == 3, w2, meta)
    meta_ref[...] = meta


def _mixer(x, modall, n1g, win, wa2, ba, glag, hgg, lbl, wout, n2g, wr, br):
    bsz, seq, dm = x.shape
    tps = MIX_TILES_PER_STEP
    tile = tps * TOK_TILE
    spt = seq // tile
    grid = (bsz, spt)
    const = lambda shape: pl.BlockSpec(shape, lambda b, s: (0,) * len(shape))
    tok = lambda width: pl.BlockSpec((1, tile, width), lambda b, s: (b, s, 0))
    return pl.pallas_call(
        _mixer_kernel,
        grid=grid,
        in_specs=[tok(dm),
                  pl.BlockSpec((1, 8, dm), lambda b, s: (b, 0, 0)),
                  const((1, dm)), const((dm, C_END)), const((LANES, GLA_QK)), const((1, GLA_QK)),
                  const((1, GLA_DV)), const((1, HG_DV)), const((2, HG_W)),
                  const((dm, dm)), const((1, dm)), const((dm, LANES)), const((1, LANES))],
        out_specs=[tok(dm), tok(dm), tok(LANES),
                   pl.BlockSpec((tps, 8, LANES), lambda b, s: (b * spt + s, 0, 0)),
                   const((1, LANES))],
        out_shape=[jax.ShapeDtypeStruct((bsz, seq, dm), F32),
                   jax.ShapeDtypeStruct((bsz, seq, dm), BF16),
                   jax.ShapeDtypeStruct((bsz, seq, LANES), F32),
                   jax.ShapeDtypeStruct((bsz * spt * tps, 8, LANES), F32),
                   jax.ShapeDtypeStruct((1, LANES), F32)],
        scratch_shapes=[pltpu.VMEM((tps, TOK_TILE, C_END), F32),
                        pltpu.VMEM((tps, TOK_TILE, dm), BF16),
                        pltpu.VMEM((GLA_HEADS, GLA_DV, GLA_DK), F32),
                        pltpu.VMEM((HG_HEADS, HG_DV, HG_DK), F32),
                        pltpu.VMEM((1, LANES), F32)],
        compiler_params=pltpu.CompilerParams(
            dimension_semantics=("arbitrary", "arbitrary"), vmem_limit_bytes=VMEM_LIMIT),
        name="mixer",
    )(x, modall, n1g, win, wa2, ba, glag, hgg, lbl, wout, n2g, wr, br)


def _local_positions(meta, rows):
    lane = lax.broadcasted_iota(jnp.int32, (meta.shape[0], rows), 1)
    p1 = meta[:, 0:1].astype(jnp.int32)
    p2 = meta[:, 1:2].astype(jnp.int32)
    return lane == p1, lane == p2


def _dispatch_kernel(tails_ref, np_ref, dst_ref, meta_ref, h2_ref, xs_ref, xl_ref, zero_ref, sem, zsem):
    @pl.when(pl.program_id(0) == 0)
    def _():
        zero_ref[...] = jnp.zeros_like(zero_ref)

        def zero_copy(e):
            return pltpu.make_async_copy(
                zero_ref, xs_ref.at[pl.ds(pl.multiple_of(tails_ref[e], ROW_GRAIN), EXP_TILE), :], zsem)

        for e in range(N_EXPERTS):
            @pl.when(tails_ref[e] >= 0)
            def _():
                zero_copy(e).start()
        for e in range(N_EXPERTS):
            @pl.when(tails_ref[e] >= 0)
            def _():
                zero_copy(e).wait()

    step = pl.program_id(0)
    slot = step % 2

    sel1, sel2 = _local_positions(meta_ref[...], LOCAL_ROWS)
    xl_ref[slot] = _dot_tn((sel1 | sel2).astype(BF16), h2_ref[...]).astype(BF16)

    def piece(j, buf, dst_row):
        src = xl_ref.at[buf, pl.ds(pl.multiple_of(j * ROW_GRAIN, ROW_GRAIN), ROW_GRAIN), :]
        dst = xs_ref.at[pl.ds(pl.multiple_of(dst_row, ROW_GRAIN), ROW_GRAIN), :]
        return pltpu.make_async_copy(src, dst, sem.at[buf])

    def issue(j, carry):
        piece(j, slot, dst_ref[j]).start()
        return carry

    lax.fori_loop(0, np_ref[step], issue, 0)

    def drain(buf):
        def body(j, carry):
            piece(j, buf, 0).wait()
            return carry
        return body

    @pl.when(step > 0)
    def _():
        lax.fori_loop(0, np_ref[step - 1], drain(1 - slot), 0)

    @pl.when(step == pl.num_programs(0) - 1)
    def _():
        lax.fori_loop(0, np_ref[step], drain(slot), 0)


def _dispatch(tails, n_pieces, dst, meta, h2, n_rows):
    n, dm = h2.shape
    tile = TOK_TILE
    return pl.pallas_call(
        _dispatch_kernel,
        grid_spec=pltpu.PrefetchScalarGridSpec(
            num_scalar_prefetch=2,
            grid=(n // tile,),
            in_specs=[pl.BlockSpec((LANES,), lambda i, tl, npc: (i,), memory_space=pltpu.SMEM),
                      pl.BlockSpec((tile, LANES), lambda i, tl, npc: (i, 0)),
                      pl.BlockSpec((tile, dm), lambda i, tl, npc: (i, 0))],
            out_specs=pl.BlockSpec(memory_space=pl.ANY),
            scratch_shapes=[pltpu.VMEM((2, LOCAL_ROWS, dm), BF16),
                            pltpu.VMEM((EXP_TILE, dm), BF16),
                            pltpu.SemaphoreType.DMA((2,)), pltpu.SemaphoreType.DMA(())]),
        out_shape=jax.ShapeDtypeStruct((n_rows, dm), BF16),
        compiler_params=pltpu.CompilerParams(
            dimension_semantics=("arbitrary",), vmem_limit_bytes=VMEM_LIMIT),
        name="dispatch",
    )(tails, n_pieces, dst, meta, h2)


def _experts_kernel(te_ref, nu_ref, xs_ref, wg_ref, wu_ref, wd_ref, ys_ref):
    i = pl.program_id(0)

    @pl.when(i < nu_ref[0])
    def _():
        xb = xs_ref[...]
        g = _dot(xb, wg_ref[0].astype(BF16))
        u = _dot(xb, wu_ref[0].astype(BF16))
        a = (_silu(g) * u).astype(BF16)
        ys_ref[...] = _dot(a, wd_ref[0].astype(BF16)).astype(BF16)


def _experts(tile_expert, n_used, xs, wg, wu, wd, n_tiles):
    rows, dm = xs.shape
    tm = EXP_TILE
    ff = wg.shape[-1]

    def row_map(i, te, nu):
        return (jnp.minimum(i, nu[0] - 1), 0)

    def out_map(i, te, nu):
        return (jnp.where(i < nu[0], i, n_tiles), 0)

    return pl.pallas_call(
        _experts_kernel,
        grid_spec=pltpu.PrefetchScalarGridSpec(
            num_scalar_prefetch=2,
            grid=(n_tiles,),
            in_specs=[pl.BlockSpec((tm, dm), row_map),
                      pl.BlockSpec((1, dm, ff), lambda i, te, nu: (te[i], 0, 0)),
                      pl.BlockSpec((1, dm, ff), lambda i, te, nu: (te[i], 0, 0)),
                      pl.BlockSpec((1, ff, dm), lambda i, te, nu: (te[i], 0, 0))],
            out_specs=pl.BlockSpec((tm, dm), out_map)),
        out_shape=jax.ShapeDtypeStruct(((n_tiles + 1) * tm, dm), BF16),
        compiler_params=pltpu.CompilerParams(
            dimension_semantics=("arbitrary",), vmem_limit_bytes=VMEM_LIMIT),
        name="experts",
    )(tile_expert, n_used, xs, wg, wu, wd)


def _combine_kernel(np_ref, src_ref, src_next_ref, x1_ref, meta_ref, mod_ref, nfg_ref, ys_ref, out_ref,
                    yl_ref, sem):
    step = pl.program_id(0) * pl.num_programs(1) + pl.program_id(1)
    last = pl.num_programs(0) * pl.num_programs(1) - 1
    slot = step % 2

    def piece(j, buf, src_row):
        src = ys_ref.at[pl.ds(pl.multiple_of(src_row, ROW_GRAIN), ROW_GRAIN), :]
        dst = yl_ref.at[buf, pl.ds(pl.multiple_of(j * ROW_GRAIN, ROW_GRAIN), ROW_GRAIN), :]
        return pltpu.make_async_copy(src, dst, sem.at[buf])

    @pl.when(step == 0)
    def _():
        yl_ref[...] = jnp.zeros_like(yl_ref)

        def first(j, carry):
            piece(j, 0, src_ref[j]).start()
            return carry

        lax.fori_loop(0, np_ref[0], first, 0)

    @pl.when(step < last)
    def _():
        def ahead(j, carry):
            piece(j, 1 - slot, src_next_ref[j]).start()
            return carry

        lax.fori_loop(0, np_ref[step + 1], ahead, 0)

    def drain(j, carry):
        piece(j, slot, 0).wait()
        return carry

    lax.fori_loop(0, np_ref[step], drain, 0)

    mod = mod_ref[0]
    gt2, shf, scf = mod[5:6], mod[6:7], mod[7:8]
    meta = meta_ref[0]
    sel1, sel2 = _local_positions(meta, LOCAL_ROWS)
    wsel = jnp.where(sel1, meta[:, 2:3], 0.0) + jnp.where(sel2, meta[:, 3:4], 0.0)
    ffn = _dot(wsel.astype(BF16), yl_ref[slot])
    x2 = x1_ref[0] + gt2 * ffn
    r = lax.rsqrt(jnp.mean(x2 * x2, axis=-1, keepdims=True) + EPS)
    out_ref[0] = (x2 * r * nfg_ref[...]) * (1.0 + scf) + shf


def _combine(n_pieces, dst, x1, meta, modall, nfg, ys):
    bsz, seq, dm = x1.shape
    tile = TOK_TILE
    spt = seq // tile
    return pl.pallas_call(
        _combine_kernel,
        grid_spec=pltpu.PrefetchScalarGridSpec(
            num_scalar_prefetch=1,
            grid=(bsz, spt),
            in_specs=[pl.BlockSpec((LANES,), lambda b, s, npc: (b * spt + s,), memory_space=pltpu.SMEM),
                      pl.BlockSpec((LANES,), lambda b, s, npc: (jnp.minimum(b * spt + s + 1, bsz * spt - 1),),
                                   memory_space=pltpu.SMEM),
                      pl.BlockSpec((1, tile, dm), lambda b, s, npc: (b, s, 0)),
                      pl.BlockSpec((1, tile, LANES), lambda b, s, npc: (b, s, 0)),
                      pl.BlockSpec((1, 8, dm), lambda b, s, npc: (b, 0, 0)),
                      pl.BlockSpec((1, dm), lambda b, s, npc: (0, 0)),
                      pl.BlockSpec(memory_space=pl.ANY)],
            out_specs=pl.BlockSpec((1, tile, dm), lambda b, s, npc: (b, s, 0)),
            scratch_shapes=[pltpu.VMEM((2, LOCAL_ROWS, dm), BF16), pltpu.SemaphoreType.DMA((2,))]),
        out_shape=jax.ShapeDtypeStruct((bsz, seq, dm), F32),
        compiler_params=pltpu.CompilerParams(
            dimension_semantics=("arbitrary", "arbitrary"), vmem_limit_bytes=VMEM_LIMIT),
        name="combine",
    )(n_pieces, dst, dst, x1, meta, modall, nfg, ys)


def kernel(x, c, w_ada, b_ada, norm1_g, w_in, gla_w_a2, gla_b_a, gla_norm_g, hg_norm_g, hg_lb_logits,
           w_out, norm2_g, w_rg, b_rg, w_re, b_re, w_exp_gate, w_exp_up, w_exp_down, w_ada_final,
           b_ada_final, norm_f_g):
    bsz, seq, dm = x.shape
    n_tok = bsz * seq

    mod = _ada(c, w_ada[0], b_ada[0], 1536)
    mod_f = _ada(c, w_ada_final, b_ada_final, 1024)
    modall = jnp.concatenate([mod.reshape(bsz, 6, dm), mod_f.reshape(bsz, 2, dm)], axis=1)

    wi = w_in[0]
    ga0 = 2 * GLA_QK + 2 * GLA_W
    win = jnp.concatenate([wi[:, :ga0],
                           jnp.pad(wi[:, ga0:ga0 + GLA_RANK], ((0, 0), (0, LANES - GLA_RANK))),
                           wi[:, ga0 + GLA_RANK:]], axis=1).astype(BF16)
    wa2 = jnp.pad(gla_w_a2[0], ((0, LANES - GLA_RANK), (0, 0))).astype(BF16)
    pad_r = LANES - N_EXPERTS - N_GROUPS
    wr = jnp.pad(jnp.concatenate([w_re[0], w_rg[0]], axis=1), ((0, 0), (0, pad_r))).astype(BF16)
    br = jnp.pad(jnp.concatenate([b_re[0], b_rg[0]]), (0, pad_r)).reshape(1, LANES)

    assert seq % (TOK_TILE * MIX_TILES_PER_STEP) == 0
    x1, h2, meta, info, total = _mixer(
        x, modall, norm1_g[0].reshape(1, dm), win, wa2, gla_b_a[0].reshape(1, GLA_QK),
        gla_norm_g[0].reshape(1, GLA_DV), hg_norm_g[0].reshape(1, HG_DV), hg_lb_logits,
        w_out[0].astype(BF16), norm2_g[0].reshape(1, dm), wr, br)

    tm = EXP_TILE
    n_tok_tiles = n_tok // TOK_TILE
    n_tiles = (2 * n_tok + n_tok_tiles * N_EXPERTS * (ROW_GRAIN - 1)) // tm + N_EXPERTS
    i32 = jnp.int32
    run = info[:, 0, :N_EXPERTS].astype(i32)
    before = info[:, 1, :N_EXPERTS].astype(i32)
    loff = info[:, 2, :N_EXPERTS].astype(i32)
    rows_e = total[0, :N_EXPERTS].astype(i32)
    region = ((rows_e + tm - 1) // tm) * tm
    ends = jnp.cumsum(region)
    tails = jnp.where(region > 0, ends - tm, -1).astype(i32)
    gbase = (ends - region)[None, :] + before
    n_pieces = (jnp.sum(run, axis=1) // ROW_GRAIN).astype(i32)
    piece_row = jnp.arange(LANES, dtype=i32) * ROW_GRAIN
    owner = jnp.sum(piece_row[None, :, None] >= (loff + run)[:, None, :], axis=-1)
    owner = jnp.minimum(owner, N_EXPERTS - 1)
    shift = gbase - loff
    pick = owner[..., None] == jnp.arange(N_EXPERTS, dtype=i32)
    dst = (jnp.sum(jnp.where(pick, shift[:, None, :], 0), axis=-1) + piece_row[None, :])
    dst = jnp.clip(dst, 0, n_tiles * tm - ROW_GRAIN).astype(i32).reshape(n_tok_tiles * LANES)
    n_used = (ends[-1] // tm).astype(i32).reshape(1)
    tile_start = jnp.arange(n_tiles, dtype=i32) * tm
    tile_start = jnp.minimum(tile_start, ends[-1] - tm)
    tile_expert = jnp.sum(tile_start[:, None] >= ends[None, :], axis=1).astype(i32)

    meta2 = meta.reshape(n_tok, LANES)
    xs = _dispatch(tails, n_pieces, dst, meta2, h2.reshape(n_tok, dm), n_tiles * tm)
    ys = _experts(tile_expert, n_used, xs, w_exp_gate[0], w_exp_up[0], w_exp_down[0], n_tiles)
    return _combine(n_pieces, dst, x1, meta, modall, norm_f_g.reshape(1, dm), ys)
```

```python
import functools

import jax
import jax.numpy as jnp
from jax import lax
from jax.experimental import pallas as pl
from jax.experimental.pallas import tpu as pltpu

F32 = jnp.float32
BF16 = jnp.bfloat16

D_MODEL = 1024
GLA_HEADS, GLA_DK, GLA_DV, GLA_RANK, GLA_TAU = 4, 64, 128, 16, 16.0
HG_HEADS, HG_DK, HG_DV = 4, 128, 128
GLA_QK = GLA_HEADS * GLA_DK
GLA_W = GLA_HEADS * GLA_DV
HG_W = HG_HEADS * HG_DK
CHUNK = 128
N_GROUPS, EXPERTS_PER_GROUP, N_EXPERTS, EXPERT_FF = 4, 8, 32, 256
EPS = 1e-6
LANES = 128

C_GQ, C_GK, C_GV, C_GG, C_GA = 0, 256, 512, 1024, 1536
C_HQ, C_HF, C_HI, C_HG, C_END = 1664, 2176, 2688, 3200, 3712

TOK_TILE = 256
MIX_TILES_PER_STEP = 2
EXP_TILE = 512
ROW_GRAIN = 16
LOCAL_ROWS = 2 * TOK_TILE + N_EXPERTS * ROW_GRAIN
MAX_PIECES = LOCAL_ROWS // ROW_GRAIN
EXP_CLAMP = 80.0
VMEM_LIMIT = 56 * 1024 * 1024


def _silu(t):
    return t * jax.nn.sigmoid(t)


def _dot(a, b):
    return jnp.dot(a, b, preferred_element_type=F32)


def _dot_nt(a, b):
    return lax.dot_general(a, b, (((1,), (1,)), ((), ())), preferred_element_type=F32)


def _dot_tn(a, b):
    return lax.dot_general(a, b, (((0,), (0,)), ((), ())), preferred_element_type=F32)


def _ada_kernel(c_ref, w_ref, b_ref, o_ref):
    ca = _silu(c_ref[...]).astype(BF16)
    o_ref[...] = _dot(ca, w_ref[...].astype(BF16)) + b_ref[...]


def _ada(c, w, b, tn):
    bsz, dm = c.shape
    n = w.shape[1]
    return pl.pallas_call(
        _ada_kernel,
        grid=(n // tn,),
        in_specs=[pl.BlockSpec((bsz, dm), lambda j: (0, 0)),
                  pl.BlockSpec((dm, tn), lambda j: (0, j)),
                  pl.BlockSpec((1, tn), lambda j: (0, j))],
        out_specs=pl.BlockSpec((bsz, tn), lambda j: (0, j)),
        out_shape=jax.ShapeDtypeStruct((bsz, n), F32),
        compiler_params=pltpu.CompilerParams(vmem_limit_bytes=VMEM_LIMIT),
        name="ada",
    )(c, w, b.reshape(1, n))


def _split2(t):
    hi = t.astype(BF16)
    lo = (t - hi.astype(F32)).astype(BF16)
    return hi, lo


def _recurrence(q, k, v, ld, st_ref, blocktril, causal, heads, dk, dv):
    t, w = q.shape
    nc = t // CHUNK
    ld_hi, ld_lo = _split2(ld)
    b = (_dot(blocktril, ld_hi) + _dot(blocktril, ld_lo)).reshape(nc, CHUNK, w)
    bm = b[:, CHUNK // 2 - 1:CHUNK // 2, :]
    bl = b[:, CHUNK - 1:CHUNK, :]
    qt = q.reshape(nc, CHUNK, w) * jnp.exp(jnp.minimum(b - bm, EXP_CLAMP))
    kt = k.reshape(nc, CHUNK, w) * jnp.exp(jnp.minimum(bm - b, EXP_CLAMP))
    qs = (qt * jnp.exp(bm)).astype(BF16).reshape(t, w)
    ks = (kt * jnp.exp(bl - bm)).astype(BF16).reshape(t, w)
    qt = qt.astype(BF16).reshape(t, w)
    kt = kt.astype(BF16).reshape(t, w)
    dec = jnp.exp(bl)
    vb = v.astype(BF16)
    states = [st_ref[h] for h in range(heads)]
    out_rows = []
    yield
    for c in range(nc):
        rs = slice(c * CHUNK, (c + 1) * CHUNK)
        outs = []
        for h in range(heads):
            ks_ = slice(h * dk, (h + 1) * dk)
            vs_ = slice(h * dv, (h + 1) * dv)
            attn = _dot_nt(qt[rs, ks_], kt[rs, ks_])
            attn = jnp.where(causal, attn, 0.0).astype(BF16)
            outs.append(_dot(attn, vb[rs, vs_]) + _dot_nt(qs[rs, ks_], states[h].astype(BF16)))
            states[h] = states[h] * dec[c][:, ks_] + _dot_tn(vb[rs, vs_], ks[rs, ks_])
        out_rows.append(jnp.concatenate(outs, axis=-1))
        yield
    for h in range(heads):
        st_ref[h] = states[h]
    return jnp.concatenate(out_rows, axis=0)


def _head_norm(o, g, heads, dv):
    outs = []
    for h in range(heads):
        oh = o[:, h * dv:(h + 1) * dv]
        r = lax.rsqrt(jnp.mean(oh * oh, axis=-1, keepdims=True) + EPS)
        outs.append(oh * r * g)
    return jnp.concatenate(outs, axis=-1)


def _mixer_kernel(x_ref, mod_ref, n1g_ref, win_ref, wa2_ref, ba_ref, glag_ref, hgg_ref, lbl_ref,
                  wout_ref, n2g_ref, wr_ref, br_ref,
                  x1_ref, h2_ref, meta_ref, info_ref, cnt_ref,
                  sg_ref, sh_ref, carry_ref, *tile_scratch):
    proj_refs = tile_scratch[:MIX_TILES_PER_STEP]
    o_refs = tile_scratch[MIX_TILES_PER_STEP:]
    bi = pl.program_id(0)
    si = pl.program_id(1)

    @pl.when(si == 0)
    def _():
        sg_ref[...] = jnp.zeros_like(sg_ref)
        sh_ref[...] = jnp.zeros_like(sh_ref)

    @pl.when((bi == 0) & (si == 0))
    def _():
        carry_ref[...] = jnp.zeros_like(carry_ref)

    n_tiles = x_ref.shape[1] // TOK_TILE
    rows = [slice(ti * TOK_TILE, (ti + 1) * TOK_TILE) for ti in range(n_tiles)]
    proj_gens = [_in_proj(x_ref.at[0, rows[ti], :], mod_ref, n1g_ref, win_ref, proj_refs[ti])
                 for ti in range(n_tiles)]
    rest_gens = [_mixer_tile(x_ref.at[0, rows[ti], :], mod_ref, wa2_ref, ba_ref, glag_ref, hgg_ref,
                             lbl_ref, wout_ref, n2g_ref, wr_ref, br_ref,
                             x1_ref.at[0, rows[ti], :], h2_ref.at[0, rows[ti], :],
                             meta_ref.at[0, rows[ti], :], info_ref.at[ti], cnt_ref,
                             proj_refs[ti], o_refs[ti], sg_ref, sh_ref, carry_ref)
                 for ti in range(n_tiles)]
    for ti in range(n_tiles):
        _interleave([proj_gens[ti]])
        _interleave([rest_gens[ti]])


def _interleave(gens):
    live = list(gens)
    while live:
        for g in list(live):
            try:
                next(g)
            except StopIteration:
                live.remove(g)


IN_PROJ_BLOCK = C_END


def _in_proj(x_ref, mod_ref, n1g_ref, win_ref, proj_ref):
    mod = mod_ref[0]
    sh1, sc1 = mod[0:1], mod[1:2]
    x = x_ref[...]
    r1 = lax.rsqrt(jnp.mean(x * x, axis=-1, keepdims=True) + EPS)
    hb = ((x * r1 * n1g_ref[...]) * (1.0 + sc1) + sh1).astype(BF16)
    for c0 in range(0, C_END, IN_PROJ_BLOCK):
        c1 = min(c0 + IN_PROJ_BLOCK, C_END)
        proj_ref[:, c0:c1] = _dot(hb, win_ref[:, c0:c1])
        yield


def _mixer_tile(x_ref, mod_ref, wa2_ref, ba_ref, glag_ref, hgg_ref, lbl_ref,
                wout_ref, n2g_ref, wr_ref, br_ref,
                x1_ref, h2_ref, meta_ref, info_ref, cnt_ref,
                proj_ref, o_ref, sg_ref, sh_ref, carry_ref):
    tile = TOK_TILE
    mod = mod_ref[0]
    gt1, sh2, sc2 = mod[2:3], mod[3:4], mod[4:5]

    a_logit = _dot(proj_ref[:, C_GA:C_GA + LANES].astype(BF16), wa2_ref[...]) + ba_ref[...]
    log_sig = jnp.minimum(a_logit, 0.0) - jnp.log(1.0 + jnp.exp(-jnp.abs(a_logit)))
    ld_g = log_sig * (1.0 / GLA_TAU)

    lbl = lbl_ref[...]
    lmax = jnp.max(lbl, axis=0, keepdims=True)
    lexp = jnp.exp(lbl - lmax)
    lb = lexp[0:1] / jnp.sum(lexp, axis=0, keepdims=True)

    row = lax.broadcasted_iota(jnp.int32, (CHUNK, CHUNK), 0)
    col = lax.broadcasted_iota(jnp.int32, (CHUNK, CHUNK), 1)
    causal = row >= col
    trow = lax.broadcasted_iota(jnp.int32, (tile, tile), 0)
    tcol = lax.broadcasted_iota(jnp.int32, (tile, tile), 1)
    same_chunk = (trow - tcol) <= (trow & (CHUNK - 1))
    blocktril = ((trow >= tcol) & same_chunk).astype(BF16)

    q = proj_ref[:, C_GQ:C_GK] * (GLA_DK ** -0.5)
    o_a = yield from _recurrence(q, proj_ref[:, C_GK:C_GV], proj_ref[:, C_GV:C_GG], ld_g, sg_ref,
                                 blocktril, causal, GLA_HEADS, GLA_DK, GLA_DV)
    o_a = _head_norm(o_a, glag_ref[...], GLA_HEADS, GLA_DV) * _silu(proj_ref[:, C_GG:C_GA])
    o_ref[:, 0:GLA_W] = o_a.astype(BF16)
    yield
    hq = proj_ref[:, C_HQ:C_HF] * (HG_DK ** -0.5)
    sig = jax.nn.sigmoid(proj_ref[:, C_HF:C_HI])
    f = lb + (1.0 - lb) * sig
    kh = (1.0 - lb) * (1.0 - sig)
    o_b = yield from _recurrence(hq, kh, proj_ref[:, C_HI:C_HG], jnp.log(f), sh_ref,
                                 blocktril, causal, HG_HEADS, HG_DK, HG_DV)
    o_b = _head_norm(o_b, hgg_ref[...], HG_HEADS, HG_DV) * _silu(proj_ref[:, C_HG:C_END])
    o_ref[:, GLA_W:GLA_W + HG_W] = o_b.astype(BF16)
    yield

    mix = _dot(o_ref[...], wout_ref[...])
    x1 = x_ref[...] + gt1 * mix
    x1_ref[...] = x1
    r2 = lax.rsqrt(jnp.mean(x1 * x1, axis=-1, keepdims=True) + EPS)
    h2 = (x1 * r2 * n2g_ref[...]) * (1.0 + sc2) + sh2
    h2b = h2.astype(BF16)
    h2_ref[...] = h2b
    yield

    logits = _dot(h2b, wr_ref[...]) + br_ref[...]
    lane = lax.broadcasted_iota(jnp.int32, (tile, LANES), 1)
    neg = jnp.float32(-jnp.inf)
    big = jnp.int32(1 << 20)
    gl = jnp.where((lane >= N_EXPERTS) & (lane < N_EXPERTS + N_GROUPS), logits, neg)
    gmax = jnp.max(gl, axis=-1, keepdims=True)
    gidx = jnp.min(jnp.where(gl == gmax, lane, big), axis=-1, keepdims=True) - N_EXPERTS
    g_p = 1.0 / jnp.sum(jnp.exp(gl - gmax), axis=-1, keepdims=True)
    in_group = (lane >= gidx * EXPERTS_PER_GROUP) & (lane < (gidx + 1) * EXPERTS_PER_GROUP)
    el = jnp.where(in_group, logits, neg)
    m1 = jnp.max(el, axis=-1, keepdims=True)
    i1 = jnp.min(jnp.where(el == m1, lane, big), axis=-1, keepdims=True)
    el2 = jnp.where(lane == i1, neg, el)
    m2 = jnp.max(el2, axis=-1, keepdims=True)
    i2 = jnp.min(jnp.where(el2 == m2, lane, big), axis=-1, keepdims=True)
    t = jnp.exp(m2 - m1)
    w1 = g_p / (1.0 + t)
    w2 = g_p * t / (1.0 + t)
    yield

    sel1 = lane == i1
    sel2 = lane == i2
    onehot = (sel1 | sel2).astype(BF16)
    trow = lax.broadcasted_iota(jnp.int32, (tile, tile), 0)
    tcol = lax.broadcasted_iota(jnp.int32, (tile, tile), 1)
    strict = (trow > tcol).astype(BF16)
    lcum = _dot(strict, onehot)
    cnt = jnp.sum(onehot.astype(F32), axis=0, keepdims=True)
    run = jnp.floor((cnt + (ROW_GRAIN - 1)) * (1.0 / ROW_GRAIN)) * ROW_GRAIN
    erow = lax.broadcasted_iota(jnp.int32, (LANES, LANES), 0)
    ecol = lax.broadcasted_iota(jnp.int32, (LANES, LANES), 1)
    before = (erow < ecol).astype(BF16)
    run8 = jnp.broadcast_to(run, (8, LANES))
    loff = _dot(run8.astype(BF16), before)
    pos = lcum + loff[0:1]
    p1 = jnp.sum(jnp.where(sel1, pos, 0.0), axis=-1, keepdims=True)
    p2 = jnp.sum(jnp.where(sel2, pos, 0.0), axis=-1, keepdims=True)

    srow = lax.broadcasted_iota(jnp.int32, (8, LANES), 0)
    info = jnp.where(srow == 0, run8, jnp.where(srow == 1, carry_ref[...], jnp.where(srow == 2, loff, 0.0)))
    info_ref[...] = info
    carry_ref[...] = carry_ref[...] + run
    cnt_ref[...] = carry_ref[...]

    meta = jnp.where(lane == 0, p1, 0.0)
    meta = jnp.where(lane == 1, p2, meta)
    meta = jnp.where(lane == 2, w1, meta)
    meta = jnp.where(lane == 3, w2, meta)
    meta_ref[...] = meta


def _mixer(x, modall, n1g, win, wa2, ba, glag, hgg, lbl, wout, n2g, wr, br):
    bsz, seq, dm = x.shape
    tps = MIX_TILES_PER_STEP
    tile = tps * TOK_TILE
    spt = seq // tile
    grid = (bsz, spt)
    const = lambda shape: pl.BlockSpec(shape, lambda b, s: (0,) * len(shape))
    tok = lambda width: pl.BlockSpec((1, tile, width), lambda b, s: (b, s, 0))
    return pl.pallas_call(
        _mixer_kernel,
        grid=grid,
        in_specs=[tok(dm),
                  pl.BlockSpec((1, 8, dm), lambda b, s: (b, 0, 0)),
                  const((1, dm)), const((dm, C_END)), const((LANES, GLA_QK)), const((1, GLA_QK)),
                  const((1, GLA_DV)), const((1, HG_DV)), const((2, HG_W)),
                  const((dm, dm)), const((1, dm)), const((dm, LANES)), const((1, LANES))],
        out_specs=[tok(dm), tok(dm), tok(LANES),
                   pl.BlockSpec((tps, 8, LANES), lambda b, s: (b * spt + s, 0, 0)),
                   const((1, LANES))],
        out_shape=[jax.ShapeDtypeStruct((bsz, seq, dm), F32),
                   jax.ShapeDtypeStruct((bsz, seq, dm), BF16),
                   jax.ShapeDtypeStruct((bsz, seq, LANES), F32),
                   jax.ShapeDtypeStruct((bsz * spt * tps, 8, LANES), F32),
                   jax.ShapeDtypeStruct((1, LANES), F32)],
        scratch_shapes=([pltpu.VMEM((GLA_HEADS, GLA_DV, GLA_DK), F32),
                         pltpu.VMEM((HG_HEADS, HG_DV, HG_DK), F32),
                         pltpu.VMEM((1, LANES), F32)]
                        + [pltpu.VMEM((TOK_TILE, C_END), F32)] * tps
                        + [pltpu.VMEM((TOK_TILE, dm), BF16)] * tps),
        compiler_params=pltpu.CompilerParams(
            dimension_semantics=("arbitrary", "arbitrary"), vmem_limit_bytes=VMEM_LIMIT),
        name="mixer",
    )(x, modall, n1g, win, wa2, ba, glag, hgg, lbl, wout, n2g, wr, br)


def _local_positions(meta, rows):
    lane = lax.broadcasted_iota(jnp.int32, (meta.shape[0], rows), 1)
    p1 = meta[:, 0:1].astype(jnp.int32)
    p2 = meta[:, 1:2].astype(jnp.int32)
    return lane == p1, lane == p2


def _dispatch_kernel(tails_ref, np_ref, dst_ref, meta_ref, h2_ref, xs_ref, xl_ref, zero_ref, sem, zsem):
    @pl.when(pl.program_id(0) == 0)
    def _():
        zero_ref[...] = jnp.zeros_like(zero_ref)

        def zero_copy(e):
            return pltpu.make_async_copy(
                zero_ref, xs_ref.at[pl.ds(pl.multiple_of(tails_ref[e], ROW_GRAIN), EXP_TILE), :], zsem)

        for e in range(N_EXPERTS):
            @pl.when(tails_ref[e] >= 0)
            def _():
                zero_copy(e).start()
        for e in range(N_EXPERTS):
            @pl.when(tails_ref[e] >= 0)
            def _():
                zero_copy(e).wait()

    step = pl.program_id(0)
    slot = step % 2

    sel1, sel2 = _local_positions(meta_ref[...], LOCAL_ROWS)
    xl_ref[slot] = _dot_tn((sel1 | sel2).astype(BF16), h2_ref[...]).astype(BF16)

    def piece(j, buf, dst_row):
        src = xl_ref.at[buf, pl.ds(pl.multiple_of(j * ROW_GRAIN, ROW_GRAIN), ROW_GRAIN), :]
        dst = xs_ref.at[pl.ds(pl.multiple_of(dst_row, ROW_GRAIN), ROW_GRAIN), :]
        return pltpu.make_async_copy(src, dst, sem.at[buf])

    def issue(j, carry):
        piece(j, slot, dst_ref[j]).start()
        return carry

    lax.fori_loop(0, np_ref[step], issue, 0)

    def drain(buf):
        def body(j, carry):
            piece(j, buf, 0).wait()
            return carry
        return body

    @pl.when(step > 0)
    def _():
        lax.fori_loop(0, np_ref[step - 1], drain(1 - slot), 0)

    @pl.when(step == pl.num_programs(0) - 1)
    def _():
        lax.fori_loop(0, np_ref[step], drain(slot), 0)


def _dispatch(tails, n_pieces, dst, meta, h2, n_rows):
    n, dm = h2.shape
    tile = TOK_TILE
    return pl.pallas_call(
        _dispatch_kernel,
        grid_spec=pltpu.PrefetchScalarGridSpec(
            num_scalar_prefetch=2,
            grid=(n // tile,),
            in_specs=[pl.BlockSpec((LANES,), lambda i, tl, npc: (i,), memory_space=pltpu.SMEM),
                      pl.BlockSpec((tile, LANES), lambda i, tl, npc: (i, 0)),
                      pl.BlockSpec((tile, dm), lambda i, tl, npc: (i, 0))],
            out_specs=pl.BlockSpec(memory_space=pl.ANY),
            scratch_shapes=[pltpu.VMEM((2, LOCAL_ROWS, dm), BF16),
                            pltpu.VMEM((EXP_TILE, dm), BF16),
                            pltpu.SemaphoreType.DMA((2,)), pltpu.SemaphoreType.DMA(())]),
        out_shape=jax.ShapeDtypeStruct((n_rows, dm), BF16),
        compiler_params=pltpu.CompilerParams(
            dimension_semantics=("arbitrary",), vmem_limit_bytes=VMEM_LIMIT),
        name="dispatch",
    )(tails, n_pieces, dst, meta, h2)


def _experts_kernel(te_ref, nu_ref, xs_ref, wg_ref, wu_ref, wd_ref, ys_ref):
    i = pl.program_id(0)

    @pl.when(i < nu_ref[0])
    def _():
        xb = xs_ref[...]
        g = _dot(xb, wg_ref[0].astype(BF16))
        u = _dot(xb, wu_ref[0].astype(BF16))
        a = (_silu(g) * u).astype(BF16)
        ys_ref[...] = _dot(a, wd_ref[0].astype(BF16)).astype(BF16)


def _experts(tile_expert, n_used, xs, wg, wu, wd, n_tiles):
    rows, dm = xs.shape
    tm = EXP_TILE
    ff = wg.shape[-1]

    def row_map(i, te, nu):
        return (jnp.minimum(i, nu[0] - 1), 0)

    def out_map(i, te, nu):
        return (jnp.where(i < nu[0], i, n_tiles), 0)

    return pl.pallas_call(
        _experts_kernel,
        grid_spec=pltpu.PrefetchScalarGridSpec(
            num_scalar_prefetch=2,
            grid=(n_tiles,),
            in_specs=[pl.BlockSpec((tm, dm), row_map),
                      pl.BlockSpec((1, dm, ff), lambda i, te, nu: (te[i], 0, 0)),
                      pl.BlockSpec((1, dm, ff), lambda i, te, nu: (te[i], 0, 0)),
                      pl.BlockSpec((1, ff, dm), lambda i, te, nu: (te[i], 0, 0))],
            out_specs=pl.BlockSpec((tm, dm), out_map)),
        out_shape=jax.ShapeDtypeStruct(((n_tiles + 1) * tm, dm), BF16),
        compiler_params=pltpu.CompilerParams(
            dimension_semantics=("arbitrary",), vmem_limit_bytes=VMEM_LIMIT),
        name="experts",
    )(tile_expert, n_used, xs, wg, wu, wd)


def _combine_kernel(np_ref, src_ref, src_next_ref, x1_ref, meta_ref, mod_ref, nfg_ref, ys_ref, out_ref,
                    yl_ref, sem):
    step = pl.program_id(0) * pl.num_programs(1) + pl.program_id(1)
    last = pl.num_programs(0) * pl.num_programs(1) - 1
    slot = step % 2

    def piece(j, buf, src_row):
        src = ys_ref.at[pl.ds(pl.multiple_of(src_row, ROW_GRAIN), ROW_GRAIN), :]
        dst = yl_ref.at[buf, pl.ds(pl.multiple_of(j * ROW_GRAIN, ROW_GRAIN), ROW_GRAIN), :]
        return pltpu.make_async_copy(src, dst, sem.at[buf])

    @pl.when(step == 0)
    def _():
        yl_ref[...] = jnp.zeros_like(yl_ref)

        def first(j, carry):
            piece(j, 0, src_ref[j]).start()
            return carry

        lax.fori_loop(0, np_ref[0], first, 0)

    @pl.when(step < last)
    def _():
        def ahead(j, carry):
            piece(j, 1 - slot, src_next_ref[j]).start()
            return carry

        lax.fori_loop(0, np_ref[step + 1], ahead, 0)

    def drain(j, carry):
        piece(j, slot, 0).wait()
        return carry

    lax.fori_loop(0, np_ref[step], drain, 0)

    mod = mod_ref[0]
    gt2, shf, scf = mod[5:6], mod[6:7], mod[7:8]
    meta = meta_ref[0]
    sel1, sel2 = _local_positions(meta, LOCAL_ROWS)
    wsel = jnp.where(sel1, meta[:, 2:3], 0.0) + jnp.where(sel2, meta[:, 3:4], 0.0)
    ffn = _dot(wsel.astype(BF16), yl_ref[slot])
    x2 = x1_ref[0] + gt2 * ffn
    r = lax.rsqrt(jnp.mean(x2 * x2, axis=-1, keepdims=True) + EPS)
    out_ref[0] = (x2 * r * nfg_ref[...]) * (1.0 + scf) + shf


def _combine(n_pieces, dst, x1, meta, modall, nfg, ys):
    bsz, seq, dm = x1.shape
    tile = TOK_TILE
    spt = seq // tile
    return pl.pallas_call(
        _combine_kernel,
        grid_spec=pltpu.PrefetchScalarGridSpec(
            num_scalar_prefetch=1,
            grid=(bsz, spt),
            in_specs=[pl.BlockSpec((LANES,), lambda b, s, npc: (b * spt + s,), memory_space=pltpu.SMEM),
                      pl.BlockSpec((LANES,), lambda b, s, npc: (jnp.minimum(b * spt + s + 1, bsz * spt - 1),),
                                   memory_space=pltpu.SMEM),
                      pl.BlockSpec((1, tile, dm), lambda b, s, npc: (b, s, 0)),
                      pl.BlockSpec((1, tile, LANES), lambda b, s, npc: (b, s, 0)),
                      pl.BlockSpec((1, 8, dm), lambda b, s, npc: (b, 0, 0)),
                      pl.BlockSpec((1, dm), lambda b, s, npc: (0, 0)),
                      pl.BlockSpec(memory_space=pl.ANY)],
            out_specs=pl.BlockSpec((1, tile, dm), lambda b, s, npc: (b, s, 0)),
            scratch_shapes=[pltpu.VMEM((2, LOCAL_ROWS, dm), BF16), pltpu.SemaphoreType.DMA((2,))]),
        out_shape=jax.ShapeDtypeStruct((bsz, seq, dm), F32),
        compiler_params=pltpu.CompilerParams(
            dimension_semantics=("arbitrary", "arbitrary"), vmem_limit_bytes=VMEM_LIMIT),
        name="combine",
    )(n_pieces, dst, dst, x1, meta, modall, nfg, ys)


def kernel(x, c, w_ada, b_ada, norm1_g, w_in, gla_w_a2, gla_b_a, gla_norm_g, hg_norm_g, hg_lb_logits,
           w_out, norm2_g, w_rg, b_rg, w_re, b_re, w_exp_gate, w_exp_up, w_exp_down, w_ada_final,
           b_ada_final, norm_f_g):
    bsz, seq, dm = x.shape
    n_tok = bsz * seq

    mod = _ada(c, w_ada[0], b_ada[0], 1536)
    mod_f = _ada(c, w_ada_final, b_ada_final, 1024)
    modall = jnp.concatenate([mod.reshape(bsz, 6, dm), mod_f.reshape(bsz, 2, dm)], axis=1)

    wi = w_in[0]
    ga0 = 2 * GLA_QK + 2 * GLA_W
    win = jnp.concatenate([wi[:, :ga0],
                           jnp.pad(wi[:, ga0:ga0 + GLA_RANK], ((0, 0), (0, LANES - GLA_RANK))),
                           wi[:, ga0 + GLA_RANK:]], axis=1).astype(BF16)
    wa2 = jnp.pad(gla_w_a2[0], ((0, LANES - GLA_RANK), (0, 0))).astype(BF16)
    pad_r = LANES - N_EXPERTS - N_GROUPS
    wr = jnp.pad(jnp.concatenate([w_re[0], w_rg[0]], axis=1), ((0, 0), (0, pad_r))).astype(BF16)
    br = jnp.pad(jnp.concatenate([b_re[0], b_rg[0]]), (0, pad_r)).reshape(1, LANES)

    assert seq % (TOK_TILE * MIX_TILES_PER_STEP) == 0
    x1, h2, meta, info, total = _mixer(
        x, modall, norm1_g[0].reshape(1, dm), win, wa2, gla_b_a[0].reshape(1, GLA_QK),
        gla_norm_g[0].reshape(1, GLA_DV), hg_norm_g[0].reshape(1, HG_DV), hg_lb_logits,
        w_out[0].astype(BF16), norm2_g[0].reshape(1, dm), wr, br)

    tm = EXP_TILE
    n_tok_tiles = n_tok // TOK_TILE
    n_tiles = (2 * n_tok + n_tok_tiles * N_EXPERTS * (ROW_GRAIN - 1)) // tm + N_EXPERTS
    i32 = jnp.int32
    run = info[:, 0, :N_EXPERTS].astype(i32)
    before = info[:, 1, :N_EXPERTS].astype(i32)
    loff = info[:, 2, :N_EXPERTS].astype(i32)
    rows_e = total[0, :N_EXPERTS].astype(i32)
    region = ((rows_e + tm - 1) // tm) * tm
    ends = jnp.cumsum(region)
    tails = jnp.where(region > 0, ends - tm, -1).astype(i32)
    gbase = (ends - region)[None, :] + before
    n_pieces = (jnp.sum(run, axis=1) // ROW_GRAIN).astype(i32)
    piece_row = jnp.arange(LANES, dtype=i32) * ROW_GRAIN
    owner = jnp.sum(piece_row[None, :, None] >= (loff + run)[:, None, :], axis=-1)
    owner = jnp.minimum(owner, N_EXPERTS - 1)
    shift = gbase - loff
    pick = owner[..., None] == jnp.arange(N_EXPERTS, dtype=i32)
    dst = (jnp.sum(jnp.where(pick, shift[:, None, :], 0), axis=-1) + piece_row[None, :])
    dst = jnp.clip(dst, 0, n_tiles * tm - ROW_GRAIN).astype(i32).reshape(n_tok_tiles * LANES)
    n_used = (ends[-1] // tm).astype(i32).reshape(1)
    tile_start = jnp.arange(n_tiles, dtype=i32) * tm
    tile_start = jnp.minimum(tile_start, ends[-1] - tm)
    tile_expert = jnp.sum(tile_start[:, None] >= ends[None, :], axis=1).astype(i32)

    meta2 = meta.reshape(n_tok, LANES)
    xs = _dispatch(tails, n_pieces, dst, meta2, h2.reshape(n_tok, dm), n_tiles * tm)
    ys = _experts(tile_expert, n_used, xs, w_exp_gate[0], w_exp_up[0], w_exp_down[0], n_tiles)
    return _combine(n_pieces, dst, x1, meta, modall, norm_f_g.reshape(1, dm), ys)
```

```python
import functools

import jax
import jax.numpy as jnp
from jax import lax
from jax.experimental import pallas as pl
from jax.experimental.pallas import tpu as pltpu

F32 = jnp.float32
BF16 = jnp.bfloat16

D_MODEL = 1024
GLA_HEADS, GLA_DK, GLA_DV, GLA_RANK, GLA_TAU = 4, 64, 128, 16, 16.0
HG_HEADS, HG_DK, HG_DV = 4, 128, 128
GLA_QK = GLA_HEADS * GLA_DK
GLA_W = GLA_HEADS * GLA_DV
HG_W = HG_HEADS * HG_DK
CHUNK = 128
N_GROUPS, EXPERTS_PER_GROUP, N_EXPERTS, EXPERT_FF = 4, 8, 32, 256
EPS = 1e-6
LANES = 128

C_GQ, C_GK, C_GV, C_GG, C_GA = 0, 256, 512, 1024, 1536
C_HQ, C_HF, C_HI, C_HG, C_END = 1664, 2176, 2688, 3200, 3712

TOK_TILE = 256
MIX_TILES_PER_STEP = 2
EXP_TILE = 512
ROW_GRAIN = 8
LOCAL_ROWS = 2 * TOK_TILE + N_EXPERTS * ROW_GRAIN
MAX_PIECES = LOCAL_ROWS // ROW_GRAIN
EXP_CLAMP = 80.0
VMEM_LIMIT = 56 * 1024 * 1024


def _silu(t):
    return t * jax.nn.sigmoid(t)


def _dot(a, b):
    return jnp.dot(a, b, preferred_element_type=F32)


def _dot_nt(a, b):
    return lax.dot_general(a, b, (((1,), (1,)), ((), ())), preferred_element_type=F32)


def _dot_tn(a, b):
    return lax.dot_general(a, b, (((0,), (0,)), ((), ())), preferred_element_type=F32)


def _ada_kernel(c_ref, w_ref, b_ref, o_ref):
    ca = _silu(c_ref[...]).astype(BF16)
    o_ref[...] = _dot(ca, w_ref[...].astype(BF16)) + b_ref[...]


def _ada(c, w, b, tn):
    bsz, dm = c.shape
    n = w.shape[1]
    return pl.pallas_call(
        _ada_kernel,
        grid=(n // tn,),
        in_specs=[pl.BlockSpec((bsz, dm), lambda j: (0, 0)),
                  pl.BlockSpec((dm, tn), lambda j: (0, j)),
                  pl.BlockSpec((1, tn), lambda j: (0, j))],
        out_specs=pl.BlockSpec((bsz, tn), lambda j: (0, j)),
        out_shape=jax.ShapeDtypeStruct((bsz, n), F32),
        compiler_params=pltpu.CompilerParams(vmem_limit_bytes=VMEM_LIMIT),
        name="ada",
    )(c, w, b.reshape(1, n))


def _split2(t):
    hi = t.astype(BF16)
    lo = (t - hi.astype(F32)).astype(BF16)
    return hi, lo


def _recurrence(q, k, v, ld, st_ref, blocktril, causal, heads, dk, dv):
    t, w = q.shape
    nc = t // CHUNK
    ld_hi, ld_lo = _split2(ld)
    b = (_dot(blocktril, ld_hi) + _dot(blocktril, ld_lo)).reshape(nc, CHUNK, w)
    bm = b[:, CHUNK // 2 - 1:CHUNK // 2, :]
    bl = b[:, CHUNK - 1:CHUNK, :]
    qt = q.reshape(nc, CHUNK, w) * jnp.exp(jnp.minimum(b - bm, EXP_CLAMP))
    kt = k.reshape(nc, CHUNK, w) * jnp.exp(jnp.minimum(bm - b, EXP_CLAMP))
    qs = (qt * jnp.exp(bm)).astype(BF16).reshape(t, w)
    ks = (kt * jnp.exp(bl - bm)).astype(BF16).reshape(t, w)
    qt = qt.astype(BF16).reshape(t, w)
    kt = kt.astype(BF16).reshape(t, w)
    dec = jnp.exp(bl)
    vb = v.astype(BF16)
    states = [st_ref[h] for h in range(heads)]
    out_rows = []
    yield
    for c in range(nc):
        rs = slice(c * CHUNK, (c + 1) * CHUNK)
        outs = []
        for h in range(heads):
            ks_ = slice(h * dk, (h + 1) * dk)
            vs_ = slice(h * dv, (h + 1) * dv)
            attn = _dot_nt(qt[rs, ks_], kt[rs, ks_])
            attn = jnp.where(causal, attn, 0.0).astype(BF16)
            outs.append(_dot(attn, vb[rs, vs_]) + _dot_nt(qs[rs, ks_], states[h].astype(BF16)))
            states[h] = states[h] * dec[c][:, ks_] + _dot_tn(vb[rs, vs_], ks[rs, ks_])
        out_rows.append(jnp.concatenate(outs, axis=-1))
        yield
    for h in range(heads):
        st_ref[h] = states[h]
    return jnp.concatenate(out_rows, axis=0)


def _head_norm(o, g, heads, dv):
    outs = []
    for h in range(heads):
        oh = o[:, h * dv:(h + 1) * dv]
        r = lax.rsqrt(jnp.mean(oh * oh, axis=-1, keepdims=True) + EPS)
        outs.append(oh * r * g)
    return jnp.concatenate(outs, axis=-1)


def _mixer_kernel(x_ref, mod_ref, n1g_ref, win_ref, wa2_ref, ba_ref, glag_ref, hgg_ref, lbl_ref,
                  wout_ref, n2g_ref, wr_ref, br_ref,
                  x1_ref, h2_ref, meta_ref, info_ref, cnt_ref,
                  sg_ref, sh_ref, carry_ref, proj_ref, o_ref):
    bi = pl.program_id(0)
    si = pl.program_id(1)

    @pl.when(si == 0)
    def _():
        sg_ref[...] = jnp.zeros_like(sg_ref)
        sh_ref[...] = jnp.zeros_like(sh_ref)

    @pl.when((bi == 0) & (si == 0))
    def _():
        carry_ref[...] = jnp.zeros_like(carry_ref)

    mod = mod_ref[0]
    sh1, sc1 = mod[0:1], mod[1:2]
    for ti in range(x_ref.shape[1] // TOK_TILE):
        rows = slice(ti * TOK_TILE, (ti + 1) * TOK_TILE)
        x = x_ref[0, rows, :]
        r1 = lax.rsqrt(jnp.mean(x * x, axis=-1, keepdims=True) + EPS)
        hb = ((x * r1 * n1g_ref[...]) * (1.0 + sc1) + sh1).astype(BF16)
        proj_ref[rows, :] = _dot(hb, win_ref[...])
        for _ in _mixer_tile(x_ref.at[0, rows, :], mod_ref, wa2_ref, ba_ref, glag_ref, hgg_ref,
                             lbl_ref, wout_ref, n2g_ref, wr_ref, br_ref,
                             x1_ref.at[0, rows, :], h2_ref.at[0, rows, :], meta_ref.at[0, rows, :],
                             info_ref.at[ti], cnt_ref, proj_ref.at[rows, :], o_ref.at[rows, :],
                             sg_ref, sh_ref, carry_ref):
            pass


def _mixer_tile(x_ref, mod_ref, wa2_ref, ba_ref, glag_ref, hgg_ref, lbl_ref,
                wout_ref, n2g_ref, wr_ref, br_ref,
                x1_ref, h2_ref, meta_ref, info_ref, cnt_ref,
                proj_ref, o_ref, sg_ref, sh_ref, carry_ref):
    tile = TOK_TILE
    mod = mod_ref[0]
    gt1, sh2, sc2 = mod[2:3], mod[3:4], mod[4:5]

    a_logit = _dot(proj_ref[:, C_GA:C_GA + LANES].astype(BF16), wa2_ref[...]) + ba_ref[...]
    log_sig = jnp.minimum(a_logit, 0.0) - jnp.log(1.0 + jnp.exp(-jnp.abs(a_logit)))
    ld_g = log_sig * (1.0 / GLA_TAU)

    lbl = lbl_ref[...]
    lmax = jnp.max(lbl, axis=0, keepdims=True)
    lexp = jnp.exp(lbl - lmax)
    lb = lexp[0:1] / jnp.sum(lexp, axis=0, keepdims=True)

    row = lax.broadcasted_iota(jnp.int32, (CHUNK, CHUNK), 0)
    col = lax.broadcasted_iota(jnp.int32, (CHUNK, CHUNK), 1)
    causal = row >= col
    trow = lax.broadcasted_iota(jnp.int32, (tile, tile), 0)
    tcol = lax.broadcasted_iota(jnp.int32, (tile, tile), 1)
    same_chunk = (trow - tcol) <= (trow & (CHUNK - 1))
    blocktril = ((trow >= tcol) & same_chunk).astype(BF16)

    q = proj_ref[:, C_GQ:C_GK] * (GLA_DK ** -0.5)
    o_a = yield from _recurrence(q, proj_ref[:, C_GK:C_GV], proj_ref[:, C_GV:C_GG], ld_g, sg_ref,
                                 blocktril, causal, GLA_HEADS, GLA_DK, GLA_DV)
    o_a = _head_norm(o_a, glag_ref[...], GLA_HEADS, GLA_DV) * _silu(proj_ref[:, C_GG:C_GA])
    o_ref[:, 0:GLA_W] = o_a.astype(BF16)
    yield
    hq = proj_ref[:, C_HQ:C_HF] * (HG_DK ** -0.5)
    sig = jax.nn.sigmoid(proj_ref[:, C_HF:C_HI])
    f = lb + (1.0 - lb) * sig
    kh = (1.0 - lb) * (1.0 - sig)
    o_b = yield from _recurrence(hq, kh, proj_ref[:, C_HI:C_HG], jnp.log(f), sh_ref,
                                 blocktril, causal, HG_HEADS, HG_DK, HG_DV)
    o_b = _head_norm(o_b, hgg_ref[...], HG_HEADS, HG_DV) * _silu(proj_ref[:, C_HG:C_END])
    o_ref[:, GLA_W:GLA_W + HG_W] = o_b.astype(BF16)
    yield

    mix = _dot(o_ref[...], wout_ref[...])
    x1 = x_ref[...] + gt1 * mix
    x1_ref[...] = x1
    r2 = lax.rsqrt(jnp.mean(x1 * x1, axis=-1, keepdims=True) + EPS)
    h2 = (x1 * r2 * n2g_ref[...]) * (1.0 + sc2) + sh2
    h2b = h2.astype(BF16)
    h2_ref[...] = h2b
    yield

    logits = _dot(h2b, wr_ref[...]) + br_ref[...]
    lane = lax.broadcasted_iota(jnp.int32, (tile, LANES), 1)
    neg = jnp.float32(-jnp.inf)
    big = jnp.int32(1 << 20)
    gl = jnp.where((lane >= N_EXPERTS) & (lane < N_EXPERTS + N_GROUPS), logits, neg)
    gmax = jnp.max(gl, axis=-1, keepdims=True)
    gidx = jnp.min(jnp.where(gl == gmax, lane, big), axis=-1, keepdims=True) - N_EXPERTS
    g_p = 1.0 / jnp.sum(jnp.exp(gl - gmax), axis=-1, keepdims=True)
    in_group = (lane >= gidx * EXPERTS_PER_GROUP) & (lane < (gidx + 1) * EXPERTS_PER_GROUP)
    el = jnp.where(in_group, logits, neg)
    m1 = jnp.max(el, axis=-1, keepdims=True)
    i1 = jnp.min(jnp.where(el == m1, lane, big), axis=-1, keepdims=True)
    el2 = jnp.where(lane == i1, neg, el)
    m2 = jnp.max(el2, axis=-1, keepdims=True)
    i2 = jnp.min(jnp.where(el2 == m2, lane, big), axis=-1, keepdims=True)
    t = jnp.exp(m2 - m1)
    w1 = g_p / (1.0 + t)
    w2 = g_p * t / (1.0 + t)
    yield

    sel1 = lane == i1
    sel2 = lane == i2
    onehot = (sel1 | sel2).astype(BF16)
    trow = lax.broadcasted_iota(jnp.int32, (tile, tile), 0)
    tcol = lax.broadcasted_iota(jnp.int32, (tile, tile), 1)
    strict = (trow > tcol).astype(BF16)
    lcum = _dot(strict, onehot)
    cnt = jnp.sum(onehot.astype(F32), axis=0, keepdims=True)
    run = jnp.floor((cnt + (ROW_GRAIN - 1)) * (1.0 / ROW_GRAIN)) * ROW_GRAIN
    erow = lax.broadcasted_iota(jnp.int32, (LANES, LANES), 0)
    ecol = lax.broadcasted_iota(jnp.int32, (LANES, LANES), 1)
    before = (erow < ecol).astype(BF16)
    run8 = jnp.broadcast_to(run, (8, LANES))
    loff = _dot(run8.astype(BF16), before)
    pos = lcum + loff[0:1]
    p1 = jnp.sum(jnp.where(sel1, pos, 0.0), axis=-1, keepdims=True)
    p2 = jnp.sum(jnp.where(sel2, pos, 0.0), axis=-1, keepdims=True)

    srow = lax.broadcasted_iota(jnp.int32, (8, LANES), 0)
    info = jnp.where(srow == 0, run8, jnp.where(srow == 1, carry_ref[...], jnp.where(srow == 2, loff, 0.0)))
    info_ref[...] = info
    carry_ref[...] = carry_ref[...] + run
    cnt_ref[...] = carry_ref[...]

    meta = jnp.where(lane == 0, p1, 0.0)
    meta = jnp.where(lane == 1, p2, meta)
    meta = jnp.where(lane == 2, w1, meta)
    meta = jnp.where(lane == 3, w2, meta)
    meta_ref[...] = meta


def _mixer(x, modall, n1g, win, wa2, ba, glag, hgg, lbl, wout, n2g, wr, br):
    bsz, seq, dm = x.shape
    tps = MIX_TILES_PER_STEP
    tile = tps * TOK_TILE
    spt = seq // tile
    grid = (bsz, spt)
    const = lambda shape: pl.BlockSpec(shape, lambda b, s: (0,) * len(shape))
    tok = lambda width: pl.BlockSpec((1, tile, width), lambda b, s: (b, s, 0))
    return pl.pallas_call(
        _mixer_kernel,
        grid=grid,
        in_specs=[tok(dm),
                  pl.BlockSpec((1, 8, dm), lambda b, s: (b, 0, 0)),
                  const((1, dm)), const((dm, C_END)), const((LANES, GLA_QK)), const((1, GLA_QK)),
                  const((1, GLA_DV)), const((1, HG_DV)), const((2, HG_W)),
                  const((dm, dm)), const((1, dm)), const((dm, LANES)), const((1, LANES))],
        out_specs=[tok(dm), tok(dm), tok(LANES),
                   pl.BlockSpec((tps, 8, LANES), lambda b, s: (b * spt + s, 0, 0)),
                   const((1, LANES))],
        out_shape=[jax.ShapeDtypeStruct((bsz, seq, dm), F32),
                   jax.ShapeDtypeStruct((bsz, seq, dm), BF16),
                   jax.ShapeDtypeStruct((bsz, seq, LANES), F32),
                   jax.ShapeDtypeStruct((bsz * spt * tps, 8, LANES), F32),
                   jax.ShapeDtypeStruct((1, LANES), F32)],
        scratch_shapes=([pltpu.VMEM((GLA_HEADS, GLA_DV, GLA_DK), F32),
                         pltpu.VMEM((HG_HEADS, HG_DV, HG_DK), F32),
                         pltpu.VMEM((1, LANES), F32),
                         pltpu.VMEM((tile, C_END), F32),
                         pltpu.VMEM((tile, dm), BF16)]),
        compiler_params=pltpu.CompilerParams(
            dimension_semantics=("arbitrary", "arbitrary"), vmem_limit_bytes=VMEM_LIMIT),
        name="mixer",
    )(x, modall, n1g, win, wa2, ba, glag, hgg, lbl, wout, n2g, wr, br)


def _local_positions(meta, rows):
    lane = lax.broadcasted_iota(jnp.int32, (meta.shape[0], rows), 1)
    p1 = meta[:, 0:1].astype(jnp.int32)
    p2 = meta[:, 1:2].astype(jnp.int32)
    return lane == p1, lane == p2


def _dispatch_kernel(tails_ref, np_ref, dst_ref, meta_ref, h2_ref, xs_ref, xl_ref, zero_ref, sem, zsem):
    @pl.when(pl.program_id(0) == 0)
    def _():
        zero_ref[...] = jnp.zeros_like(zero_ref)

        def zero_copy(e):
            return pltpu.make_async_copy(
                zero_ref, xs_ref.at[pl.ds(pl.multiple_of(tails_ref[e], ROW_GRAIN), EXP_TILE), :], zsem)

        for e in range(N_EXPERTS):
            @pl.when(tails_ref[e] >= 0)
            def _():
                zero_copy(e).start()
        for e in range(N_EXPERTS):
            @pl.when(tails_ref[e] >= 0)
            def _():
                zero_copy(e).wait()

    step = pl.program_id(0)
    slot = step % 2

    sel1, sel2 = _local_positions(meta_ref[...], LOCAL_ROWS)
    xl_ref[slot] = _dot_tn((sel1 | sel2).astype(BF16), h2_ref[...]).astype(BF16)

    def piece(j, buf, dst_row):
        src = xl_ref.at[buf, pl.ds(pl.multiple_of(j * ROW_GRAIN, ROW_GRAIN), ROW_GRAIN), :]
        dst = xs_ref.at[pl.ds(pl.multiple_of(dst_row, ROW_GRAIN), ROW_GRAIN), :]
        return pltpu.make_async_copy(src, dst, sem.at[buf])

    def issue(j, carry):
        piece(j, slot, dst_ref[j]).start()
        return carry

    lax.fori_loop(0, np_ref[step], issue, 0)

    def drain(buf):
        def body(j, carry):
            piece(j, buf, 0).wait()
            return carry
        return body

    @pl.when(step > 0)
    def _():
        lax.fori_loop(0, np_ref[step - 1], drain(1 - slot), 0)

    @pl.when(step == pl.num_programs(0) - 1)
    def _():
        lax.fori_loop(0, np_ref[step], drain(slot), 0)


def _dispatch(tails, n_pieces, dst, meta, h2, n_rows):
    n, dm = h2.shape
    tile = TOK_TILE
    return pl.pallas_call(
        _dispatch_kernel,
        grid_spec=pltpu.PrefetchScalarGridSpec(
            num_scalar_prefetch=2,
            grid=(n // tile,),
            in_specs=[pl.BlockSpec((LANES,), lambda i, tl, npc: (i,), memory_space=pltpu.SMEM),
                      pl.BlockSpec((tile, LANES), lambda i, tl, npc: (i, 0)),
                      pl.BlockSpec((tile, dm), lambda i, tl, npc: (i, 0))],
            out_specs=pl.BlockSpec(memory_space=pl.ANY),
            scratch_shapes=[pltpu.VMEM((2, LOCAL_ROWS, dm), BF16),
                            pltpu.VMEM((EXP_TILE, dm), BF16),
                            pltpu.SemaphoreType.DMA((2,)), pltpu.SemaphoreType.DMA(())]),
        out_shape=jax.ShapeDtypeStruct((n_rows, dm), BF16),
        compiler_params=pltpu.CompilerParams(
            dimension_semantics=("arbitrary",), vmem_limit_bytes=VMEM_LIMIT),
        name="dispatch",
    )(tails, n_pieces, dst, meta, h2)


def _experts_kernel(te_ref, nu_ref, xs_ref, wg_ref, wu_ref, wd_ref, ys_ref):
    i = pl.program_id(0)

    @pl.when(i < nu_ref[0])
    def _():
        xb = xs_ref[...]
        g = _dot(xb, wg_ref[0].astype(BF16))
        u = _dot(xb, wu_ref[0].astype(BF16))
        a = (_silu(g) * u).astype(BF16)
        ys_ref[...] = _dot(a, wd_ref[0].astype(BF16)).astype(BF16)


def _experts(tile_expert, n_used, xs, wg, wu, wd, n_tiles):
    rows, dm = xs.shape
    tm = EXP_TILE
    ff = wg.shape[-1]

    def row_map(i, te, nu):
        return (jnp.minimum(i, nu[0] - 1), 0)

    def out_map(i, te, nu):
        return (jnp.where(i < nu[0], i, n_tiles), 0)

    return pl.pallas_call(
        _experts_kernel,
        grid_spec=pltpu.PrefetchScalarGridSpec(
            num_scalar_prefetch=2,
            grid=(n_tiles,),
            in_specs=[pl.BlockSpec((tm, dm), row_map),
                      pl.BlockSpec((1, dm, ff), lambda i, te, nu: (te[i], 0, 0)),
                      pl.BlockSpec((1, dm, ff), lambda i, te, nu: (te[i], 0, 0)),
                      pl.BlockSpec((1, ff, dm), lambda i, te, nu: (te[i], 0, 0))],
            out_specs=pl.BlockSpec((tm, dm), out_map)),
        out_shape=jax.ShapeDtypeStruct(((n_tiles + 1) * tm, dm), BF16),
        compiler_params=pltpu.CompilerParams(
            dimension_semantics=("arbitrary",), vmem_limit_bytes=VMEM_LIMIT),
        name="experts",
    )(tile_expert, n_used, xs, wg, wu, wd)


def _combine_kernel(np_ref, src_ref, src_next_ref, x1_ref, meta_ref, mod_ref, nfg_ref, ys_ref, out_ref,
                    yl_ref, sem):
    step = pl.program_id(0) * pl.num_programs(1) + pl.program_id(1)
    last = pl.num_programs(0) * pl.num_programs(1) - 1
    slot = step % 2

    def piece(j, buf, src_row):
        src = ys_ref.at[pl.ds(pl.multiple_of(src_row, ROW_GRAIN), ROW_GRAIN), :]
        dst = yl_ref.at[buf, pl.ds(pl.multiple_of(j * ROW_GRAIN, ROW_GRAIN), ROW_GRAIN), :]
        return pltpu.make_async_copy(src, dst, sem.at[buf])

    @pl.when(step == 0)
    def _():
        yl_ref[...] = jnp.zeros_like(yl_ref)

        def first(j, carry):
            piece(j, 0, src_ref[j]).start()
            return carry

        lax.fori_loop(0, np_ref[0], first, 0)

    @pl.when(step < last)
    def _():
        def ahead(j, carry):
            piece(j, 1 - slot, src_next_ref[j]).start()
            return carry

        lax.fori_loop(0, np_ref[step + 1], ahead, 0)

    def drain(j, carry):
        piece(j, slot, 0).wait()
        return carry

    lax.fori_loop(0, np_ref[step], drain, 0)

    mod = mod_ref[0]
    gt2, shf, scf = mod[5:6], mod[6:7], mod[7:8]
    meta = meta_ref[0]
    sel1, sel2 = _local_positions(meta, LOCAL_ROWS)
    wsel = jnp.where(sel1, meta[:, 2:3], 0.0) + jnp.where(sel2, meta[:, 3:4], 0.0)
    ffn = _dot(wsel.astype(BF16), yl_ref[slot])
    x2 = x1_ref[0] + gt2 * ffn
    r = lax.rsqrt(jnp.mean(x2 * x2, axis=-1, keepdims=True) + EPS)
    out_ref[0] = (x2 * r * nfg_ref[...]) * (1.0 + scf) + shf


def _combine(n_pieces, dst, x1, meta, modall, nfg, ys):
    bsz, seq, dm = x1.shape
    tile = TOK_TILE
    spt = seq // tile
    return pl.pallas_call(
        _combine_kernel,
        grid_spec=pltpu.PrefetchScalarGridSpec(
            num_scalar_prefetch=1,
            grid=(bsz, spt),
            in_specs=[pl.BlockSpec((LANES,), lambda b, s, npc: (b * spt + s,), memory_space=pltpu.SMEM),
                      pl.BlockSpec((LANES,), lambda b, s, npc: (jnp.minimum(b * spt + s + 1, bsz * spt - 1),),
                                   memory_space=pltpu.SMEM),
                      pl.BlockSpec((1, tile, dm), lambda b, s, npc: (b, s, 0)),
                      pl.BlockSpec((1, tile, LANES), lambda b, s, npc: (b, s, 0)),
                      pl.BlockSpec((1, 8, dm), lambda b, s, npc: (b, 0, 0)),
                      pl.BlockSpec((1, dm), lambda b, s, npc: (0, 0)),
                      pl.BlockSpec(memory_space=pl.ANY)],
            out_specs=pl.BlockSpec((1, tile, dm), lambda b, s, npc: (b, s, 0)),
            scratch_shapes=[pltpu.VMEM((2, LOCAL_ROWS, dm), BF16), pltpu.SemaphoreType.DMA((2,))]),
        out_shape=jax.ShapeDtypeStruct((bsz, seq, dm), F32),
        compiler_params=pltpu.CompilerParams(
            dimension_semantics=("arbitrary", "arbitrary"), vmem_limit_bytes=VMEM_LIMIT),
        name="combine",
    )(n_pieces, dst, dst, x1, meta, modall, nfg, ys)


def kernel(x, c, w_ada, b_ada, norm1_g, w_in, gla_w_a2, gla_b_a, gla_norm_g, hg_norm_g, hg_lb_logits,
           w_out, norm2_g, w_rg, b_rg, w_re, b_re, w_exp_gate, w_exp_up, w_exp_down, w_ada_final,
           b_ada_final, norm_f_g):
    bsz, seq, dm = x.shape
    n_tok = bsz * seq

    mod = _ada(c, w_ada[0], b_ada[0], 1536)
    mod_f = _ada(c, w_ada_final, b_ada_final, 1024)
    modall = jnp.concatenate([mod.reshape(bsz, 6, dm), mod_f.reshape(bsz, 2, dm)], axis=1)

    wi = w_in[0]
    ga0 = 2 * GLA_QK + 2 * GLA_W
    win = jnp.concatenate([wi[:, :ga0],
                           jnp.pad(wi[:, ga0:ga0 + GLA_RANK], ((0, 0), (0, LANES - GLA_RANK))),
                           wi[:, ga0 + GLA_RANK:]], axis=1).astype(BF16)
    wa2 = jnp.pad(gla_w_a2[0], ((0, LANES - GLA_RANK), (0, 0))).astype(BF16)
    pad_r = LANES - N_EXPERTS - N_GROUPS
    wr = jnp.pad(jnp.concatenate([w_re[0], w_rg[0]], axis=1), ((0, 0), (0, pad_r))).astype(BF16)
    br = jnp.pad(jnp.concatenate([b_re[0], b_rg[0]]), (0, pad_r)).reshape(1, LANES)

    assert seq % (TOK_TILE * MIX_TILES_PER_STEP) == 0
    x1, h2, meta, info, total = _mixer(
        x, modall, norm1_g[0].reshape(1, dm), win, wa2, gla_b_a[0].reshape(1, GLA_QK),
        gla_norm_g[0].reshape(1, GLA_DV), hg_norm_g[0].reshape(1, HG_DV), hg_lb_logits,
        w_out[0].astype(BF16), norm2_g[0].reshape(1, dm), wr, br)

    tm = EXP_TILE
    n_tok_tiles = n_tok // TOK_TILE
    n_tiles = (2 * n_tok + n_tok_tiles * N_EXPERTS * (ROW_GRAIN - 1)) // tm + N_EXPERTS
    i32 = jnp.int32
    run = info[:, 0, :N_EXPERTS].astype(i32)
    before = info[:, 1, :N_EXPERTS].astype(i32)
    loff = info[:, 2, :N_EXPERTS].astype(i32)
    rows_e = total[0, :N_EXPERTS].astype(i32)
    region = ((rows_e + tm - 1) // tm) * tm
    ends = jnp.cumsum(region)
    tails = jnp.where(region > 0, ends - tm, -1).astype(i32)
    gbase = (ends - region)[None, :] + before
    n_pieces = (jnp.sum(run, axis=1) // ROW_GRAIN).astype(i32)
    piece_row = jnp.arange(LANES, dtype=i32) * ROW_GRAIN
    owner = jnp.sum(piece_row[None, :, None] >= (loff + run)[:, None, :], axis=-1)
    owner = jnp.minimum(owner, N_EXPERTS - 1)
    shift = gbase - loff
    pick = owner[..., None] == jnp.arange(N_EXPERTS, dtype=i32)
    dst = (jnp.sum(jnp.where(pick, shift[:, None, :], 0), axis=-1) + piece_row[None, :])
    dst = jnp.clip(dst, 0, n_tiles * tm - ROW_GRAIN).astype(i32).reshape(n_tok_tiles * LANES)
    n_used = (ends[-1] // tm).astype(i32).reshape(1)
    tile_start = jnp.arange(n_tiles, dtype=i32) * tm
    tile_start = jnp.minimum(tile_start, ends[-1] - tm)
    tile_expert = jnp.sum(tile_start[:, None] >= ends[None, :], axis=1).astype(i32)

    meta2 = meta.reshape(n_tok, LANES)
    xs = _dispatch(tails, n_pieces, dst, meta2, h2.reshape(n_tok, dm), n_tiles * tm)
    ys = _experts(tile_expert, n_used, xs, w_exp_gate[0], w_exp_up[0], w_exp_down[0], n_tiles)
    return _combine(n_pieces, dst, x1, meta, modall, norm_f_g.reshape(1, dm), ys)
```

```python
import functools

import jax
import jax.numpy as jnp
from jax import lax
from jax.experimental import pallas as pl
from jax.experimental.pallas import tpu as pltpu

F32 = jnp.float32
BF16 = jnp.bfloat16

D_MODEL = 1024
GLA_HEADS, GLA_DK, GLA_DV, GLA_RANK, GLA_TAU = 4, 64, 128, 16, 16.0
HG_HEADS, HG_DK, HG_DV = 4, 128, 128
GLA_QK = GLA_HEADS * GLA_DK
GLA_W = GLA_HEADS * GLA_DV
HG_W = HG_HEADS * HG_DK
CHUNK = 128
N_GROUPS, EXPERTS_PER_GROUP, N_EXPERTS, EXPERT_FF = 4, 8, 32, 256
EPS = 1e-6
LANES = 128

C_GQ, C_GK, C_GV, C_GG, C_GA = 0, 256, 512, 1024, 1536
C_HQ, C_HF, C_HI, C_HG, C_END = 1664, 2176, 2688, 3200, 3712

TOK_TILE = 256
MIX_TILES_PER_STEP = 2
EXP_TILE = 1024
ROW_GRAIN = 16
LOCAL_ROWS = 2 * TOK_TILE + N_EXPERTS * ROW_GRAIN
MAX_PIECES = LOCAL_ROWS // ROW_GRAIN
EXP_CLAMP = 80.0
VMEM_LIMIT = 56 * 1024 * 1024


def _silu(t):
    return t * jax.nn.sigmoid(t)


def _dot(a, b):
    return jnp.dot(a, b, preferred_element_type=F32)


def _dot_nt(a, b):
    return lax.dot_general(a, b, (((1,), (1,)), ((), ())), preferred_element_type=F32)


def _dot_tn(a, b):
    return lax.dot_general(a, b, (((0,), (0,)), ((), ())), preferred_element_type=F32)


def _ada_kernel(c_ref, w_ref, b_ref, o_ref):
    ca = _silu(c_ref[...]).astype(BF16)
    o_ref[...] = _dot(ca, w_ref[...].astype(BF16)) + b_ref[...]


def _ada(c, w, b, tn):
    bsz, dm = c.shape
    n = w.shape[1]
    return pl.pallas_call(
        _ada_kernel,
        grid=(n // tn,),
        in_specs=[pl.BlockSpec((bsz, dm), lambda j: (0, 0)),
                  pl.BlockSpec((dm, tn), lambda j: (0, j)),
                  pl.BlockSpec((1, tn), lambda j: (0, j))],
        out_specs=pl.BlockSpec((bsz, tn), lambda j: (0, j)),
        out_shape=jax.ShapeDtypeStruct((bsz, n), F32),
        compiler_params=pltpu.CompilerParams(vmem_limit_bytes=VMEM_LIMIT),
        name="ada",
    )(c, w, b.reshape(1, n))


def _split2(t):
    hi = t.astype(BF16)
    lo = (t - hi.astype(F32)).astype(BF16)
    return hi, lo


def _recurrence(q, k, v, ld, st_ref, blocktril, causal, heads, dk, dv):
    t, w = q.shape
    nc = t // CHUNK
    ld_hi, ld_lo = _split2(ld)
    b = (_dot(blocktril, ld_hi) + _dot(blocktril, ld_lo)).reshape(nc, CHUNK, w)
    bm = b[:, CHUNK // 2 - 1:CHUNK // 2, :]
    bl = b[:, CHUNK - 1:CHUNK, :]
    qt = q.reshape(nc, CHUNK, w) * jnp.exp(jnp.minimum(b - bm, EXP_CLAMP))
    kt = k.reshape(nc, CHUNK, w) * jnp.exp(jnp.minimum(bm - b, EXP_CLAMP))
    qs = (qt * jnp.exp(bm)).astype(BF16).reshape(t, w)
    ks = (kt * jnp.exp(bl - bm)).astype(BF16).reshape(t, w)
    qt = qt.astype(BF16).reshape(t, w)
    kt = kt.astype(BF16).reshape(t, w)
    dec = jnp.exp(bl)
    vb = v.astype(BF16)
    states = [st_ref[h] for h in range(heads)]
    out_rows = []
    yield
    for c in range(nc):
        rs = slice(c * CHUNK, (c + 1) * CHUNK)
        outs = []
        for h in range(heads):
            ks_ = slice(h * dk, (h + 1) * dk)
            vs_ = slice(h * dv, (h + 1) * dv)
            attn = _dot_nt(qt[rs, ks_], kt[rs, ks_])
            attn = jnp.where(causal, attn, 0.0).astype(BF16)
            outs.append(_dot(attn, vb[rs, vs_]) + _dot_nt(qs[rs, ks_], states[h].astype(BF16)))
            states[h] = states[h] * dec[c][:, ks_] + _dot_tn(vb[rs, vs_], ks[rs, ks_])
        out_rows.append(jnp.concatenate(outs, axis=-1))
        yield
    for h in range(heads):
        st_ref[h] = states[h]
    return jnp.concatenate(out_rows, axis=0)


def _head_norm(o, g, heads, dv):
    outs = []
    for h in range(heads):
        oh = o[:, h * dv:(h + 1) * dv]
        r = lax.rsqrt(jnp.mean(oh * oh, axis=-1, keepdims=True) + EPS)
        outs.append(oh * r * g)
    return jnp.concatenate(outs, axis=-1)


def _mixer_kernel(x_ref, mod_ref, n1g_ref, win_ref, wa2_ref, ba_ref, glag_ref, hgg_ref, lbl_ref,
                  wout_ref, n2g_ref, wr_ref, br_ref,
                  x1_ref, h2_ref, meta_ref, info_ref, cnt_ref,
                  sg_ref, sh_ref, carry_ref, proj_ref, o_ref):
    bi = pl.program_id(0)
    si = pl.program_id(1)

    @pl.when(si == 0)
    def _():
        sg_ref[...] = jnp.zeros_like(sg_ref)
        sh_ref[...] = jnp.zeros_like(sh_ref)

    @pl.when((bi == 0) & (si == 0))
    def _():
        carry_ref[...] = jnp.zeros_like(carry_ref)

    mod = mod_ref[0]
    sh1, sc1 = mod[0:1], mod[1:2]
    for ti in range(x_ref.shape[1] // TOK_TILE):
        rows = slice(ti * TOK_TILE, (ti + 1) * TOK_TILE)
        x = x_ref[0, rows, :]
        r1 = lax.rsqrt(jnp.mean(x * x, axis=-1, keepdims=True) + EPS)
        hb = ((x * r1 * n1g_ref[...]) * (1.0 + sc1) + sh1).astype(BF16)
        proj_ref[rows, :] = _dot(hb, win_ref[...])
        for _ in _mixer_tile(x_ref.at[0, rows, :], mod_ref, wa2_ref, ba_ref, glag_ref, hgg_ref,
                             lbl_ref, wout_ref, n2g_ref, wr_ref, br_ref,
                             x1_ref.at[0, rows, :], h2_ref.at[0, rows, :], meta_ref.at[0, rows, :],
                             info_ref.at[ti], cnt_ref, proj_ref.at[rows, :], o_ref.at[rows, :],
                             sg_ref, sh_ref, carry_ref):
            pass


def _mixer_tile(x_ref, mod_ref, wa2_ref, ba_ref, glag_ref, hgg_ref, lbl_ref,
                wout_ref, n2g_ref, wr_ref, br_ref,
                x1_ref, h2_ref, meta_ref, info_ref, cnt_ref,
                proj_ref, o_ref, sg_ref, sh_ref, carry_ref):
    tile = TOK_TILE
    mod = mod_ref[0]
    gt1, sh2, sc2 = mod[2:3], mod[3:4], mod[4:5]

    a_logit = _dot(proj_ref[:, C_GA:C_GA + LANES].astype(BF16), wa2_ref[...]) + ba_ref[...]
    log_sig = jnp.minimum(a_logit, 0.0) - jnp.log(1.0 + jnp.exp(-jnp.abs(a_logit)))
    ld_g = log_sig * (1.0 / GLA_TAU)

    lbl = lbl_ref[...]
    lmax = jnp.max(lbl, axis=0, keepdims=True)
    lexp = jnp.exp(lbl - lmax)
    lb = lexp[0:1] / jnp.sum(lexp, axis=0, keepdims=True)

    row = lax.broadcasted_iota(jnp.int32, (CHUNK, CHUNK), 0)
    col = lax.broadcasted_iota(jnp.int32, (CHUNK, CHUNK), 1)
    causal = row >= col
    trow = lax.broadcasted_iota(jnp.int32, (tile, tile), 0)
    tcol = lax.broadcasted_iota(jnp.int32, (tile, tile), 1)
    same_chunk = (trow - tcol) <= (trow & (CHUNK - 1))
    blocktril = ((trow >= tcol) & same_chunk).astype(BF16)

    q = proj_ref[:, C_GQ:C_GK] * (GLA_DK ** -0.5)
    o_a = yield from _recurrence(q, proj_ref[:, C_GK:C_GV], proj_ref[:, C_GV:C_GG], ld_g, sg_ref,
                                 blocktril, causal, GLA_HEADS, GLA_DK, GLA_DV)
    o_a = _head_norm(o_a, glag_ref[...], GLA_HEADS, GLA_DV) * _silu(proj_ref[:, C_GG:C_GA])
    o_ref[:, 0:GLA_W] = o_a.astype(BF16)
    yield
    hq = proj_ref[:, C_HQ:C_HF] * (HG_DK ** -0.5)
    sig = jax.nn.sigmoid(proj_ref[:, C_HF:C_HI])
    f = lb + (1.0 - lb) * sig
    kh = (1.0 - lb) * (1.0 - sig)
    o_b = yield from _recurrence(hq, kh, proj_ref[:, C_HI:C_HG], jnp.log(f), sh_ref,
                                 blocktril, causal, HG_HEADS, HG_DK, HG_DV)
    o_b = _head_norm(o_b, hgg_ref[...], HG_HEADS, HG_DV) * _silu(proj_ref[:, C_HG:C_END])
    o_ref[:, GLA_W:GLA_W + HG_W] = o_b.astype(BF16)
    yield

    mix = _dot(o_ref[...], wout_ref[...])
    x1 = x_ref[...] + gt1 * mix
    x1_ref[...] = x1
    r2 = lax.rsqrt(jnp.mean(x1 * x1, axis=-1, keepdims=True) + EPS)
    h2 = (x1 * r2 * n2g_ref[...]) * (1.0 + sc2) + sh2
    h2b = h2.astype(BF16)
    h2_ref[...] = h2b
    yield

    logits = _dot(h2b, wr_ref[...]) + br_ref[...]
    lane = lax.broadcasted_iota(jnp.int32, (tile, LANES), 1)
    neg = jnp.float32(-jnp.inf)
    big = jnp.int32(1 << 20)
    gl = jnp.where((lane >= N_EXPERTS) & (lane < N_EXPERTS + N_GROUPS), logits, neg)
    gmax = jnp.max(gl, axis=-1, keepdims=True)
    gidx = jnp.min(jnp.where(gl == gmax, lane, big), axis=-1, keepdims=True) - N_EXPERTS
    g_p = 1.0 / jnp.sum(jnp.exp(gl - gmax), axis=-1, keepdims=True)
    in_group = (lane >= gidx * EXPERTS_PER_GROUP) & (lane < (gidx + 1) * EXPERTS_PER_GROUP)
    el = jnp.where(in_group, logits, neg)
    m1 = jnp.max(el, axis=-1, keepdims=True)
    i1 = jnp.min(jnp.where(el == m1, lane, big), axis=-1, keepdims=True)
    el2 = jnp.where(lane == i1, neg, el)
    m2 = jnp.max(el2, axis=-1, keepdims=True)
    i2 = jnp.min(jnp.where(el2 == m2, lane, big), axis=-1, keepdims=True)
    t = jnp.exp(m2 - m1)
    w1 = g_p / (1.0 + t)
    w2 = g_p * t / (1.0 + t)
    yield

    sel1 = lane == i1
    sel2 = lane == i2
    onehot = (sel1 | sel2).astype(BF16)
    trow = lax.broadcasted_iota(jnp.int32, (tile, tile), 0)
    tcol = lax.broadcasted_iota(jnp.int32, (tile, tile), 1)
    strict = (trow > tcol).astype(BF16)
    lcum = _dot(strict, onehot)
    cnt = jnp.sum(onehot.astype(F32), axis=0, keepdims=True)
    run = jnp.floor((cnt + (ROW_GRAIN - 1)) * (1.0 / ROW_GRAIN)) * ROW_GRAIN
    erow = lax.broadcasted_iota(jnp.int32, (LANES, LANES), 0)
    ecol = lax.broadcasted_iota(jnp.int32, (LANES, LANES), 1)
    before = (erow < ecol).astype(BF16)
    run8 = jnp.broadcast_to(run, (8, LANES))
    loff = _dot(run8.astype(BF16), before)
    pos = lcum + loff[0:1]
    p1 = jnp.sum(jnp.where(sel1, pos, 0.0), axis=-1, keepdims=True)
    p2 = jnp.sum(jnp.where(sel2, pos, 0.0), axis=-1, keepdims=True)

    srow = lax.broadcasted_iota(jnp.int32, (8, LANES), 0)
    info = jnp.where(srow == 0, run8, jnp.where(srow == 1, carry_ref[...], jnp.where(srow == 2, loff, 0.0)))
    info_ref[...] = info
    carry_ref[...] = carry_ref[...] + run
    cnt_ref[...] = carry_ref[...]

    meta = jnp.where(lane == 0, p1, 0.0)
    meta = jnp.where(lane == 1, p2, meta)
    meta = jnp.where(lane == 2, w1, meta)
    meta = jnp.where(lane == 3, w2, meta)
    meta_ref[...] = meta


def _mixer(x, modall, n1g, win, wa2, ba, glag, hgg, lbl, wout, n2g, wr, br):
    bsz, seq, dm = x.shape
    tps = MIX_TILES_PER_STEP
    tile = tps * TOK_TILE
    spt = seq // tile
    grid = (bsz, spt)
    const = lambda shape: pl.BlockSpec(shape, lambda b, s: (0,) * len(shape))
    tok = lambda width: pl.BlockSpec((1, tile, width), lambda b, s: (b, s, 0))
    return pl.pallas_call(
        _mixer_kernel,
        grid=grid,
        in_specs=[tok(dm),
                  pl.BlockSpec((1, 8, dm), lambda b, s: (b, 0, 0)),
                  const((1, dm)), const((dm, C_END)), const((LANES, GLA_QK)), const((1, GLA_QK)),
                  const((1, GLA_DV)), const((1, HG_DV)), const((2, HG_W)),
                  const((dm, dm)), const((1, dm)), const((dm, LANES)), const((1, LANES))],
        out_specs=[tok(dm), tok(dm), tok(LANES),
                   pl.BlockSpec((tps, 8, LANES), lambda b, s: (b * spt + s, 0, 0)),
                   const((1, LANES))],
        out_shape=[jax.ShapeDtypeStruct((bsz, seq, dm), F32),
                   jax.ShapeDtypeStruct((bsz, seq, dm), BF16),
                   jax.ShapeDtypeStruct((bsz, seq, LANES), F32),
                   jax.ShapeDtypeStruct((bsz * spt * tps, 8, LANES), F32),
                   jax.ShapeDtypeStruct((1, LANES), F32)],
        scratch_shapes=([pltpu.VMEM((GLA_HEADS, GLA_DV, GLA_DK), F32),
                         pltpu.VMEM((HG_HEADS, HG_DV, HG_DK), F32),
                         pltpu.VMEM((1, LANES), F32),
                         pltpu.VMEM((tile, C_END), F32),
                         pltpu.VMEM((tile, dm), BF16)]),
        compiler_params=pltpu.CompilerParams(
            dimension_semantics=("arbitrary", "arbitrary"), vmem_limit_bytes=VMEM_LIMIT),
        name="mixer",
    )(x, modall, n1g, win, wa2, ba, glag, hgg, lbl, wout, n2g, wr, br)


def _local_positions(meta, rows):
    lane = lax.broadcasted_iota(jnp.int32, (meta.shape[0], rows), 1)
    p1 = meta[:, 0:1].astype(jnp.int32)
    p2 = meta[:, 1:2].astype(jnp.int32)
    return lane == p1, lane == p2


def _dispatch_kernel(tails_ref, np_ref, dst_ref, meta_ref, h2_ref, xs_ref, xl_ref, zero_ref, sem, zsem):
    @pl.when(pl.program_id(0) == 0)
    def _():
        zero_ref[...] = jnp.zeros_like(zero_ref)

        def zero_copy(e):
            return pltpu.make_async_copy(
                zero_ref, xs_ref.at[pl.ds(pl.multiple_of(tails_ref[e], ROW_GRAIN), EXP_TILE), :], zsem)

        for e in range(N_EXPERTS):
            @pl.when(tails_ref[e] >= 0)
            def _():
                zero_copy(e).start()
        for e in range(N_EXPERTS):
            @pl.when(tails_ref[e] >= 0)
            def _():
                zero_copy(e).wait()

    step = pl.program_id(0)
    slot = step % 2

    sel1, sel2 = _local_positions(meta_ref[...], LOCAL_ROWS)
    xl_ref[slot] = _dot_tn((sel1 | sel2).astype(BF16), h2_ref[...]).astype(BF16)

    def piece(j, buf, dst_row):
        src = xl_ref.at[buf, pl.ds(pl.multiple_of(j * ROW_GRAIN, ROW_GRAIN), ROW_GRAIN), :]
        dst = xs_ref.at[pl.ds(pl.multiple_of(dst_row, ROW_GRAIN), ROW_GRAIN), :]
        return pltpu.make_async_copy(src, dst, sem.at[buf])

    def issue(j, carry):
        piece(j, slot, dst_ref[j]).start()
        return carry

    lax.fori_loop(0, np_ref[step], issue, 0)

    def drain(buf):
        def body(j, carry):
            piece(j, buf, 0).wait()
            return carry
        return body

    @pl.when(step > 0)
    def _():
        lax.fori_loop(0, np_ref[step - 1], drain(1 - slot), 0)

    @pl.when(step == pl.num_programs(0) - 1)
    def _():
        lax.fori_loop(0, np_ref[step], drain(slot), 0)


def _dispatch(tails, n_pieces, dst, meta, h2, n_rows):
    n, dm = h2.shape
    tile = TOK_TILE
    return pl.pallas_call(
        _dispatch_kernel,
        grid_spec=pltpu.PrefetchScalarGridSpec(
            num_scalar_prefetch=2,
            grid=(n // tile,),
            in_specs=[pl.BlockSpec((LANES,), lambda i, tl, npc: (i,), memory_space=pltpu.SMEM),
                      pl.BlockSpec((tile, LANES), lambda i, tl, npc: (i, 0)),
                      pl.BlockSpec((tile, dm), lambda i, tl, npc: (i, 0))],
            out_specs=pl.BlockSpec(memory_space=pl.ANY),
            scratch_shapes=[pltpu.VMEM((2, LOCAL_ROWS, dm), BF16),
                            pltpu.VMEM((EXP_TILE, dm), BF16),
                            pltpu.SemaphoreType.DMA((2,)), pltpu.SemaphoreType.DMA(())]),
        out_shape=jax.ShapeDtypeStruct((n_rows, dm), BF16),
        compiler_params=pltpu.CompilerParams(
            dimension_semantics=("arbitrary",), vmem_limit_bytes=VMEM_LIMIT),
        name="dispatch",
    )(tails, n_pieces, dst, meta, h2)


def _experts_kernel(te_ref, nu_ref, xs_ref, wg_ref, wu_ref, wd_ref, ys_ref):
    i = pl.program_id(0)

    @pl.when(i < nu_ref[0])
    def _():
        xb = xs_ref[...]
        g = _dot(xb, wg_ref[0].astype(BF16))
        u = _dot(xb, wu_ref[0].astype(BF16))
        a = (_silu(g) * u).astype(BF16)
        ys_ref[...] = _dot(a, wd_ref[0].astype(BF16)).astype(BF16)


def _experts(tile_expert, n_used, xs, wg, wu, wd, n_tiles):
    rows, dm = xs.shape
    tm = EXP_TILE
    ff = wg.shape[-1]

    def row_map(i, te, nu):
        return (jnp.minimum(i, nu[0] - 1), 0)

    def out_map(i, te, nu):
        return (jnp.where(i < nu[0], i, n_tiles), 0)

    return pl.pallas_call(
        _experts_kernel,
        grid_spec=pltpu.PrefetchScalarGridSpec(
            num_scalar_prefetch=2,
            grid=(n_tiles,),
            in_specs=[pl.BlockSpec((tm, dm), row_map),
                      pl.BlockSpec((1, dm, ff), lambda i, te, nu: (te[i], 0, 0)),
                      pl.BlockSpec((1, dm, ff), lambda i, te, nu: (te[i], 0, 0)),
                      pl.BlockSpec((1, ff, dm), lambda i, te, nu: (te[i], 0, 0))],
            out_specs=pl.BlockSpec((tm, dm), out_map)),
        out_shape=jax.ShapeDtypeStruct(((n_tiles + 1) * tm, dm), BF16),
        compiler_params=pltpu.CompilerParams(
            dimension_semantics=("arbitrary",), vmem_limit_bytes=VMEM_LIMIT),
        name="experts",
    )(tile_expert, n_used, xs, wg, wu, wd)


def _combine_kernel(np_ref, src_ref, src_next_ref, x1_ref, meta_ref, mod_ref, nfg_ref, ys_ref, out_ref,
                    yl_ref, sem):
    step = pl.program_id(0) * pl.num_programs(1) + pl.program_id(1)
    last = pl.num_programs(0) * pl.num_programs(1) - 1
    slot = step % 2

    def piece(j, buf, src_row):
        src = ys_ref.at[pl.ds(pl.multiple_of(src_row, ROW_GRAIN), ROW_GRAIN), :]
        dst = yl_ref.at[buf, pl.ds(pl.multiple_of(j * ROW_GRAIN, ROW_GRAIN), ROW_GRAIN), :]
        return pltpu.make_async_copy(src, dst, sem.at[buf])

    @pl.when(step == 0)
    def _():
        yl_ref[...] = jnp.zeros_like(yl_ref)

        def first(j, carry):
            piece(j, 0, src_ref[j]).start()
            return carry

        lax.fori_loop(0, np_ref[0], first, 0)

    @pl.when(step < last)
    def _():
        def ahead(j, carry):
            piece(j, 1 - slot, src_next_ref[j]).start()
            return carry

        lax.fori_loop(0, np_ref[step + 1], ahead, 0)

    def drain(j, carry):
        piece(j, slot, 0).wait()
        return carry

    lax.fori_loop(0, np_ref[step], drain, 0)

    mod = mod_ref[0]
    gt2, shf, scf = mod[5:6], mod[6:7], mod[7:8]
    meta = meta_ref[0]
    sel1, sel2 = _local_positions(meta, LOCAL_ROWS)
    wsel = jnp.where(sel1, meta[:, 2:3], jnp.where(sel2, meta[:, 3:4], 0.0))
    ffn = _dot(wsel.astype(BF16), yl_ref[slot])
    x2 = x1_ref[0] + gt2 * ffn
    r = lax.rsqrt(jnp.mean(x2 * x2, axis=-1, keepdims=True) + EPS)
    out_ref[0] = (x2 * r * nfg_ref[...]) * (1.0 + scf) + shf


def _combine(n_pieces, dst, x1, meta, modall, nfg, ys):
    bsz, seq, dm = x1.shape
    tile = TOK_TILE
    spt = seq // tile
    return pl.pallas_call(
        _combine_kernel,
        grid_spec=pltpu.PrefetchScalarGridSpec(
            num_scalar_prefetch=1,
            grid=(bsz, spt),
            in_specs=[pl.BlockSpec((LANES,), lambda b, s, npc: (b * spt + s,), memory_space=pltpu.SMEM),
                      pl.BlockSpec((LANES,), lambda b, s, npc: (jnp.minimum(b * spt + s + 1, bsz * spt - 1),),
                                   memory_space=pltpu.SMEM),
                      pl.BlockSpec((1, tile, dm), lambda b, s, npc: (b, s, 0)),
                      pl.BlockSpec((1, tile, LANES), lambda b, s, npc: (b, s, 0)),
                      pl.BlockSpec((1, 8, dm), lambda b, s, npc: (b, 0, 0)),
                      pl.BlockSpec((1, dm), lambda b, s, npc: (0, 0)),
                      pl.BlockSpec(memory_space=pl.ANY)],
            out_specs=pl.BlockSpec((1, tile, dm), lambda b, s, npc: (b, s, 0)),
            scratch_shapes=[pltpu.VMEM((2, LOCAL_ROWS, dm), BF16), pltpu.SemaphoreType.DMA((2,))]),
        out_shape=jax.ShapeDtypeStruct((bsz, seq, dm), F32),
        compiler_params=pltpu.CompilerParams(
            dimension_semantics=("arbitrary", "arbitrary"), vmem_limit_bytes=VMEM_LIMIT),
        name="combine",
    )(n_pieces, dst, dst, x1, meta, modall, nfg, ys)


def kernel(x, c, w_ada, b_ada, norm1_g, w_in, gla_w_a2, gla_b_a, gla_norm_g, hg_norm_g, hg_lb_logits,
           w_out, norm2_g, w_rg, b_rg, w_re, b_re, w_exp_gate, w_exp_up, w_exp_down, w_ada_final,
           b_ada_final, norm_f_g):
    bsz, seq, dm = x.shape
    n_tok = bsz * seq

    mod = _ada(c, w_ada[0], b_ada[0], 1536)
    mod_f = _ada(c, w_ada_final, b_ada_final, 1024)
    modall = jnp.concatenate([mod.reshape(bsz, 6, dm), mod_f.reshape(bsz, 2, dm)], axis=1)

    wi = w_in[0]
    ga0 = 2 * GLA_QK + 2 * GLA_W
    win = jnp.concatenate([wi[:, :ga0],
                           jnp.pad(wi[:, ga0:ga0 + GLA_RANK], ((0, 0), (0, LANES - GLA_RANK))),
                           wi[:, ga0 + GLA_RANK:]], axis=1).astype(BF16)
    wa2 = jnp.pad(gla_w_a2[0], ((0, LANES - GLA_RANK), (0, 0))).astype(BF16)
    pad_r = LANES - N_EXPERTS - N_GROUPS
    wr = jnp.pad(jnp.concatenate([w_re[0], w_rg[0]], axis=1), ((0, 0), (0, pad_r))).astype(BF16)
    br = jnp.pad(jnp.concatenate([b_re[0], b_rg[0]]), (0, pad_r)).reshape(1, LANES)

    assert seq % (TOK_TILE * MIX_TILES_PER_STEP) == 0
    x1, h2, meta, info, total = _mixer(
        x, modall, norm1_g[0].reshape(1, dm), win, wa2, gla_b_a[0].reshape(1, GLA_QK),
        gla_norm_g[0].reshape(1, GLA_DV), hg_norm_g[0].reshape(1, HG_DV), hg_lb_logits,
        w_out[0].astype(BF16), norm2_g[0].reshape(1, dm), wr, br)

    tm = EXP_TILE
    n_tok_tiles = n_tok // TOK_TILE
    n_tiles = (2 * n_tok + n_tok_tiles * N_EXPERTS * (ROW_GRAIN - 1)) // tm + N_EXPERTS
    i32 = jnp.int32
    run = info[:, 0, :N_EXPERTS].astype(i32)
    before = info[:, 1, :N_EXPERTS].astype(i32)
    loff = info[:, 2, :N_EXPERTS].astype(i32)
    rows_e = total[0, :N_EXPERTS].astype(i32)
    region = ((rows_e + tm - 1) // tm) * tm
    ends = jnp.cumsum(region)
    tails = jnp.where(region > 0, ends - tm, -1).astype(i32)
    gbase = (ends - region)[None, :] + before
    n_pieces = (jnp.sum(run, axis=1) // ROW_GRAIN).astype(i32)
    piece_row = jnp.arange(LANES, dtype=i32) * ROW_GRAIN
    owner = jnp.sum(piece_row[None, :, None] >= (loff + run)[:, None, :], axis=-1)
    owner = jnp.minimum(owner, N_EXPERTS - 1)
    shift = gbase - loff
    pick = owner[..., None] == jnp.arange(N_EXPERTS, dtype=i32)
    dst = (jnp.sum(jnp.where(pick, shift[:, None, :], 0), axis=-1) + piece_row[None, :])
    dst = jnp.clip(dst, 0, n_tiles * tm - ROW_GRAIN).astype(i32).reshape(n_tok_tiles * LANES)
    n_used = (ends[-1] // tm).astype(i32).reshape(1)
    tile_start = jnp.arange(n_tiles, dtype=i32) * tm
    tile_start = jnp.minimum(tile_start, ends[-1] - tm)
    tile_expert = jnp.sum(tile_start[:, None] >= ends[None, :], axis=1).astype(i32)

    meta2 = meta.reshape(n_tok, LANES)
    xs = _dispatch(tails, n_pieces, dst, meta2, h2.reshape(n_tok, dm), n_tiles * tm)
    ys = _experts(tile_expert, n_used, xs, w_exp_gate[0], w_exp_up[0], w_exp_down[0], n_tiles)
    return _combine(n_pieces, dst, x1, meta, modall, norm_f_g.reshape(1, dm), ys)
```

```python
import functools

import jax
import jax.numpy as jnp
from jax import lax
from jax.experimental import pallas as pl
from jax.experimental.pallas import tpu as pltpu

F32 = jnp.float32
BF16 = jnp.bfloat16

D_MODEL = 1024
GLA_HEADS, GLA_DK, GLA_DV, GLA_RANK, GLA_TAU = 4, 64, 128, 16, 16.0
HG_HEADS, HG_DK, HG_DV = 4, 128, 128
GLA_QK = GLA_HEADS * GLA_DK
GLA_W = GLA_HEADS * GLA_DV
HG_W = HG_HEADS * HG_DK
CHUNK = 128
N_GROUPS, EXPERTS_PER_GROUP, N_EXPERTS, EXPERT_FF = 4, 8, 32, 256
EPS = 1e-6
LANES = 128

C_GQ, C_GK, C_GV, C_GG, C_GA = 0, 256, 512, 1024, 1536
C_HQ, C_HF, C_HI, C_HG, C_END = 1664, 2176, 2688, 3200, 3712

TOK_TILE = 512
MIX_SUB = 256
EXP_TILE = 1024
ROW_GRAIN = 16
LOCAL_ROWS = 2 * TOK_TILE + N_EXPERTS * ROW_GRAIN
MAX_PIECES = LOCAL_ROWS // ROW_GRAIN
EXP_CLAMP = 80.0
VMEM_LIMIT = 56 * 1024 * 1024


def _silu(t):
    return t * jax.nn.sigmoid(t)


def _dot(a, b):
    return jnp.dot(a, b, preferred_element_type=F32)


def _dot_nt(a, b):
    return lax.dot_general(a, b, (((1,), (1,)), ((), ())), preferred_element_type=F32)


def _dot_tn(a, b):
    return lax.dot_general(a, b, (((0,), (0,)), ((), ())), preferred_element_type=F32)


def _ada_kernel(c_ref, w_ref, b_ref, o_ref):
    ca = _silu(c_ref[...]).astype(BF16)
    o_ref[...] = _dot(ca, w_ref[...].astype(BF16)) + b_ref[...]


def _ada(c, w, b, tn):
    bsz, dm = c.shape
    n = w.shape[1]
    return pl.pallas_call(
        _ada_kernel,
        grid=(n // tn,),
        in_specs=[pl.BlockSpec((bsz, dm), lambda j: (0, 0)),
                  pl.BlockSpec((dm, tn), lambda j: (0, j)),
                  pl.BlockSpec((1, tn), lambda j: (0, j))],
        out_specs=pl.BlockSpec((bsz, tn), lambda j: (0, j)),
        out_shape=jax.ShapeDtypeStruct((bsz, n), F32),
        compiler_params=pltpu.CompilerParams(vmem_limit_bytes=VMEM_LIMIT),
        name="ada",
    )(c, w, b.reshape(1, n))


def _split2(t):
    hi = t.astype(BF16)
    lo = (t - hi.astype(F32)).astype(BF16)
    return hi, lo


def _recurrence(q, k, v, ld, st_ref, blocktril, causal, heads, dk, dv):
    t, w = q.shape
    nc = t // CHUNK
    ld_hi, ld_lo = _split2(ld)
    b = (_dot(blocktril, ld_hi) + _dot(blocktril, ld_lo)).reshape(nc, CHUNK, w)
    bm = b[:, CHUNK // 2 - 1:CHUNK // 2, :]
    bl = b[:, CHUNK - 1:CHUNK, :]
    qt = q.reshape(nc, CHUNK, w) * jnp.exp(jnp.minimum(b - bm, EXP_CLAMP))
    kt = k.reshape(nc, CHUNK, w) * jnp.exp(jnp.minimum(bm - b, EXP_CLAMP))
    qs = (qt * jnp.exp(bm)).astype(BF16).reshape(t, w)
    ks = (kt * jnp.exp(bl - bm)).astype(BF16).reshape(t, w)
    qt = qt.astype(BF16).reshape(t, w)
    kt = kt.astype(BF16).reshape(t, w)
    dec = jnp.exp(bl)
    vb = v.astype(BF16)
    states = [st_ref[h] for h in range(heads)]
    out_rows = []
    for c in range(nc):
        rs = slice(c * CHUNK, (c + 1) * CHUNK)
        outs = []
        for h in range(heads):
            ks_ = slice(h * dk, (h + 1) * dk)
            vs_ = slice(h * dv, (h + 1) * dv)
            attn = _dot_nt(qt[rs, ks_], kt[rs, ks_])
            attn = jnp.where(causal, attn, 0.0).astype(BF16)
            outs.append(_dot(attn, vb[rs, vs_]) + _dot_nt(qs[rs, ks_], states[h].astype(BF16)))
            states[h] = states[h] * dec[c][:, ks_] + _dot_tn(vb[rs, vs_], ks[rs, ks_])
        out_rows.append(jnp.concatenate(outs, axis=-1))
    for h in range(heads):
        st_ref[h] = states[h]
    return jnp.concatenate(out_rows, axis=0)


def _head_norm(o, g, heads, dv):
    outs = []
    for h in range(heads):
        oh = o[:, h * dv:(h + 1) * dv]
        r = lax.rsqrt(jnp.mean(oh * oh, axis=-1, keepdims=True) + EPS)
        outs.append(oh * r * g)
    return jnp.concatenate(outs, axis=-1)


def _mixer_kernel(x_ref, mod_ref, n1g_ref, win_ref, wa2_ref, ba_ref, glag_ref, hgg_ref, lbl_ref,
                  wout_ref, n2g_ref, wr_ref, br_ref,
                  x1_ref, h2_ref, meta_ref, info_ref, cnt_ref,
                  sg_ref, sh_ref, carry_ref, proj_ref, o_ref):
    bi = pl.program_id(0)
    si = pl.program_id(1)

    @pl.when(si == 0)
    def _():
        sg_ref[...] = jnp.zeros_like(sg_ref)
        sh_ref[...] = jnp.zeros_like(sh_ref)

    @pl.when((bi == 0) & (si == 0))
    def _():
        carry_ref[...] = jnp.zeros_like(carry_ref)

    mod = mod_ref[0]
    sh1, sc1 = mod[0:1], mod[1:2]
    for ti in range(TOK_TILE // MIX_SUB):
        rows = slice(ti * MIX_SUB, (ti + 1) * MIX_SUB)
        x = x_ref[0, rows, :]
        r1 = lax.rsqrt(jnp.mean(x * x, axis=-1, keepdims=True) + EPS)
        hb = ((x * r1 * n1g_ref[...]) * (1.0 + sc1) + sh1).astype(BF16)
        proj_ref[rows, :] = _dot(hb, win_ref[...])
        _mixer_rows(x_ref.at[0, rows, :], mod_ref, wa2_ref, ba_ref, glag_ref, hgg_ref,
                    lbl_ref, wout_ref, n2g_ref, x1_ref.at[0, rows, :], h2_ref.at[0, rows, :],
                    proj_ref.at[rows, :], o_ref.at[rows, :], sg_ref, sh_ref)
    _route(h2_ref.at[0], wr_ref, br_ref, meta_ref.at[0], info_ref.at[0], cnt_ref, carry_ref)


def _mixer_rows(x_ref, mod_ref, wa2_ref, ba_ref, glag_ref, hgg_ref, lbl_ref, wout_ref, n2g_ref,
                x1_ref, h2_ref, proj_ref, o_ref, sg_ref, sh_ref):
    tile = MIX_SUB
    mod = mod_ref[0]
    gt1, sh2, sc2 = mod[2:3], mod[3:4], mod[4:5]

    lbl = lbl_ref[...]
    lmax = jnp.max(lbl, axis=0, keepdims=True)
    lexp = jnp.exp(lbl - lmax)
    lb = lexp[0:1] / jnp.sum(lexp, axis=0, keepdims=True)

    row = lax.broadcasted_iota(jnp.int32, (CHUNK, CHUNK), 0)
    col = lax.broadcasted_iota(jnp.int32, (CHUNK, CHUNK), 1)
    causal = row >= col
    trow = lax.broadcasted_iota(jnp.int32, (tile, tile), 0)
    tcol = lax.broadcasted_iota(jnp.int32, (tile, tile), 1)
    same_chunk = (trow - tcol) <= (trow & (CHUNK - 1))
    blocktril = ((trow >= tcol) & same_chunk).astype(BF16)

    a_logit = _dot(proj_ref[:, C_GA:C_GA + LANES].astype(BF16), wa2_ref[...]) + ba_ref[...]
    log_sig = jnp.minimum(a_logit, 0.0) - jnp.log(1.0 + jnp.exp(-jnp.abs(a_logit)))
    q = proj_ref[:, C_GQ:C_GK] * (GLA_DK ** -0.5)
    o_a = _recurrence(q, proj_ref[:, C_GK:C_GV], proj_ref[:, C_GV:C_GG], log_sig * (1.0 / GLA_TAU),
                      sg_ref, blocktril, causal, GLA_HEADS, GLA_DK, GLA_DV)
    o_a = _head_norm(o_a, glag_ref[...], GLA_HEADS, GLA_DV) * _silu(proj_ref[:, C_GG:C_GA])
    o_ref[:, 0:GLA_W] = o_a.astype(BF16)
    hq = proj_ref[:, C_HQ:C_HF] * (HG_DK ** -0.5)
    sig = jax.nn.sigmoid(proj_ref[:, C_HF:C_HI])
    f = lb + (1.0 - lb) * sig
    kh = (1.0 - lb) * (1.0 - sig)
    o_b = _recurrence(hq, kh, proj_ref[:, C_HI:C_HG], jnp.log(f), sh_ref,
                      blocktril, causal, HG_HEADS, HG_DK, HG_DV)
    o_b = _head_norm(o_b, hgg_ref[...], HG_HEADS, HG_DV) * _silu(proj_ref[:, C_HG:C_END])
    o_ref[:, GLA_W:GLA_W + HG_W] = o_b.astype(BF16)

    mix = _dot(o_ref[...], wout_ref[...])
    x1 = x_ref[...] + gt1 * mix
    x1_ref[...] = x1
    r2 = lax.rsqrt(jnp.mean(x1 * x1, axis=-1, keepdims=True) + EPS)
    h2 = (x1 * r2 * n2g_ref[...]) * (1.0 + sc2) + sh2
    h2_ref[...] = h2.astype(BF16)


def _route(h2_ref, wr_ref, br_ref, meta_ref, info_ref, cnt_ref, carry_ref):
    tile = TOK_TILE
    logits = _dot(h2_ref[...], wr_ref[...]) + br_ref[...]
    lane = lax.broadcasted_iota(jnp.int32, (tile, LANES), 1)
    neg = jnp.float32(-jnp.inf)
    big = jnp.int32(1 << 20)
    gl = jnp.where((lane >= N_EXPERTS) & (lane < N_EXPERTS + N_GROUPS), logits, neg)
    gmax = jnp.max(gl, axis=-1, keepdims=True)
    gidx = jnp.min(jnp.where(gl == gmax, lane, big), axis=-1, keepdims=True) - N_EXPERTS
    g_p = 1.0 / jnp.sum(jnp.exp(gl - gmax), axis=-1, keepdims=True)
    in_group = (lane >= gidx * EXPERTS_PER_GROUP) & (lane < (gidx + 1) * EXPERTS_PER_GROUP)
    el = jnp.where(in_group, logits, neg)
    m1 = jnp.max(el, axis=-1, keepdims=True)
    i1 = jnp.min(jnp.where(el == m1, lane, big), axis=-1, keepdims=True)
    el2 = jnp.where(lane == i1, neg, el)
    m2 = jnp.max(el2, axis=-1, keepdims=True)
    i2 = jnp.min(jnp.where(el2 == m2, lane, big), axis=-1, keepdims=True)
    t = jnp.exp(m2 - m1)
    w1 = g_p / (1.0 + t)
    w2 = g_p * t / (1.0 + t)

    sel1 = lane == i1
    sel2 = lane == i2
    onehot = (sel1 | sel2).astype(BF16)
    trow = lax.broadcasted_iota(jnp.int32, (tile, tile), 0)
    tcol = lax.broadcasted_iota(jnp.int32, (tile, tile), 1)
    strict = (trow > tcol).astype(BF16)
    lcum = _dot(strict, onehot)
    cnt = jnp.sum(onehot.astype(F32), axis=0, keepdims=True)
    run = jnp.floor((cnt + (ROW_GRAIN - 1)) * (1.0 / ROW_GRAIN)) * ROW_GRAIN
    erow = lax.broadcasted_iota(jnp.int32, (LANES, LANES), 0)
    ecol = lax.broadcasted_iota(jnp.int32, (LANES, LANES), 1)
    before = (erow < ecol).astype(BF16)
    run8 = jnp.broadcast_to(run, (8, LANES))
    loff = _dot(run8.astype(BF16), before)
    pos = lcum + loff[0:1]
    p1 = jnp.sum(jnp.where(sel1, pos, 0.0), axis=-1, keepdims=True)
    p2 = jnp.sum(jnp.where(sel2, pos, 0.0), axis=-1, keepdims=True)

    srow = lax.broadcasted_iota(jnp.int32, (8, LANES), 0)
    info = jnp.where(srow == 0, run8, jnp.where(srow == 1, carry_ref[...], jnp.where(srow == 2, loff, 0.0)))
    info_ref[...] = info
    carry_ref[...] = carry_ref[...] + run
    cnt_ref[...] = carry_ref[...]

    meta = jnp.where(lane == 0, p1, 0.0)
    meta = jnp.where(lane == 1, p2, meta)
    meta = jnp.where(lane == 2, w1, meta)
    meta = jnp.where(lane == 3, w2, meta)
    meta_ref[...] = meta


def _mixer(x, modall, n1g, win, wa2, ba, glag, hgg, lbl, wout, n2g, wr, br):
    bsz, seq, dm = x.shape
    tile = TOK_TILE
    spt = seq // tile
    grid = (bsz, spt)
    const = lambda shape: pl.BlockSpec(shape, lambda b, s: (0,) * len(shape))
    tok = lambda width: pl.BlockSpec((1, tile, width), lambda b, s: (b, s, 0))
    return pl.pallas_call(
        _mixer_kernel,
        grid=grid,
        in_specs=[tok(dm),
                  pl.BlockSpec((1, 8, dm), lambda b, s: (b, 0, 0)),
                  const((1, dm)), const((dm, C_END)), const((LANES, GLA_QK)), const((1, GLA_QK)),
                  const((1, GLA_DV)), const((1, HG_DV)), const((2, HG_W)),
                  const((dm, dm)), const((1, dm)), const((dm, LANES)), const((1, LANES))],
        out_specs=[tok(dm), tok(dm), tok(LANES),
                   pl.BlockSpec((1, 8, LANES), lambda b, s: (b * spt + s, 0, 0)),
                   const((1, LANES))],
        out_shape=[jax.ShapeDtypeStruct((bsz, seq, dm), F32),
                   jax.ShapeDtypeStruct((bsz, seq, dm), BF16),
                   jax.ShapeDtypeStruct((bsz, seq, LANES), F32),
                   jax.ShapeDtypeStruct((bsz * spt, 8, LANES), F32),
                   jax.ShapeDtypeStruct((1, LANES), F32)],
        scratch_shapes=([pltpu.VMEM((GLA_HEADS, GLA_DV, GLA_DK), F32),
                         pltpu.VMEM((HG_HEADS, HG_DV, HG_DK), F32),
                         pltpu.VMEM((1, LANES), F32),
                         pltpu.VMEM((tile, C_END), F32),
                         pltpu.VMEM((tile, dm), BF16)]),
        compiler_params=pltpu.CompilerParams(
            dimension_semantics=("arbitrary", "arbitrary"), vmem_limit_bytes=VMEM_LIMIT),
        name="mixer",
    )(x, modall, n1g, win, wa2, ba, glag, hgg, lbl, wout, n2g, wr, br)


def _local_positions(meta, rows):
    lane = lax.broadcasted_iota(jnp.int32, (meta.shape[0], rows), 1)
    p1 = meta[:, 0:1].astype(jnp.int32)
    p2 = meta[:, 1:2].astype(jnp.int32)
    return lane == p1, lane == p2


def _dispatch_kernel(tails_ref, np_ref, dst_ref, meta_ref, h2_ref, xs_ref, xl_ref, zero_ref, sem, zsem):
    @pl.when(pl.program_id(0) == 0)
    def _():
        zero_ref[...] = jnp.zeros_like(zero_ref)

        def zero_copy(e):
            return pltpu.make_async_copy(
                zero_ref, xs_ref.at[pl.ds(pl.multiple_of(tails_ref[e], ROW_GRAIN), EXP_TILE), :], zsem)

        for e in range(N_EXPERTS):
            @pl.when(tails_ref[e] >= 0)
            def _():
                zero_copy(e).start()
        for e in range(N_EXPERTS):
            @pl.when(tails_ref[e] >= 0)
            def _():
                zero_copy(e).wait()

    step = pl.program_id(0)
    slot = step % 2

    sel1, sel2 = _local_positions(meta_ref[...], LOCAL_ROWS)
    xl_ref[slot] = _dot_tn((sel1 | sel2).astype(BF16), h2_ref[...]).astype(BF16)

    def piece(j, buf, dst_row):
        src = xl_ref.at[buf, pl.ds(pl.multiple_of(j * ROW_GRAIN, ROW_GRAIN), ROW_GRAIN), :]
        dst = xs_ref.at[pl.ds(pl.multiple_of(dst_row, ROW_GRAIN), ROW_GRAIN), :]
        return pltpu.make_async_copy(src, dst, sem.at[buf])

    def issue(j, carry):
        piece(j, slot, dst_ref[j]).start()
        return carry

    lax.fori_loop(0, np_ref[step], issue, 0)

    def drain(buf):
        def body(j, carry):
            piece(j, buf, 0).wait()
            return carry
        return body

    @pl.when(step > 0)
    def _():
        lax.fori_loop(0, np_ref[step - 1], drain(1 - slot), 0)

    @pl.when(step == pl.num_programs(0) - 1)
    def _():
        lax.fori_loop(0, np_ref[step], drain(slot), 0)


def _dispatch(tails, n_pieces, dst, meta, h2, n_rows):
    n, dm = h2.shape
    tile = TOK_TILE
    return pl.pallas_call(
        _dispatch_kernel,
        grid_spec=pltpu.PrefetchScalarGridSpec(
            num_scalar_prefetch=2,
            grid=(n // tile,),
            in_specs=[pl.BlockSpec((LANES,), lambda i, tl, npc: (i,), memory_space=pltpu.SMEM),
                      pl.BlockSpec((tile, LANES), lambda i, tl, npc: (i, 0)),
                      pl.BlockSpec((tile, dm), lambda i, tl, npc: (i, 0))],
            out_specs=pl.BlockSpec(memory_space=pl.ANY),
            scratch_shapes=[pltpu.VMEM((2, LOCAL_ROWS, dm), BF16),
                            pltpu.VMEM((EXP_TILE, dm), BF16),
                            pltpu.SemaphoreType.DMA((2,)), pltpu.SemaphoreType.DMA(())]),
        out_shape=jax.ShapeDtypeStruct((n_rows, dm), BF16),
        compiler_params=pltpu.CompilerParams(
            dimension_semantics=("arbitrary",), vmem_limit_bytes=VMEM_LIMIT),
        name="dispatch",
    )(tails, n_pieces, dst, meta, h2)


def _experts_kernel(te_ref, nu_ref, xs_ref, wg_ref, wu_ref, wd_ref, ys_ref):
    i = pl.program_id(0)

    @pl.when(i < nu_ref[0])
    def _():
        xb = xs_ref[...]
        g = _dot(xb, wg_ref[0].astype(BF16))
        u = _dot(xb, wu_ref[0].astype(BF16))
        a = (_silu(g) * u).astype(BF16)
        ys_ref[...] = _dot(a, wd_ref[0].astype(BF16)).astype(BF16)


def _experts(tile_expert, n_used, xs, wg, wu, wd, n_tiles):
    rows, dm = xs.shape
    tm = EXP_TILE
    ff = wg.shape[-1]

    def row_map(i, te, nu):
        return (jnp.minimum(i, nu[0] - 1), 0)

    def out_map(i, te, nu):
        return (jnp.where(i < nu[0], i, n_tiles), 0)

    return pl.pallas_call(
        _experts_kernel,
        grid_spec=pltpu.PrefetchScalarGridSpec(
            num_scalar_prefetch=2,
            grid=(n_tiles,),
            in_specs=[pl.BlockSpec((tm, dm), row_map),
                      pl.BlockSpec((1, dm, ff), lambda i, te, nu: (te[i], 0, 0)),
                      pl.BlockSpec((1, dm, ff), lambda i, te, nu: (te[i], 0, 0)),
                      pl.BlockSpec((1, ff, dm), lambda i, te, nu: (te[i], 0, 0))],
            out_specs=pl.BlockSpec((tm, dm), out_map)),
        out_shape=jax.ShapeDtypeStruct(((n_tiles + 1) * tm, dm), BF16),
        compiler_params=pltpu.CompilerParams(
            dimension_semantics=("arbitrary",), vmem_limit_bytes=VMEM_LIMIT),
        name="experts",
    )(tile_expert, n_used, xs, wg, wu, wd)


def _combine_kernel(np_ref, src_ref, src_next_ref, x1_ref, meta_ref, mod_ref, nfg_ref, ys_ref, out_ref,
                    yl_ref, sem):
    step = pl.program_id(0) * pl.num_programs(1) + pl.program_id(1)
    last = pl.num_programs(0) * pl.num_programs(1) - 1
    slot = step % 2

    def piece(j, buf, src_row):
        src = ys_ref.at[pl.ds(pl.multiple_of(src_row, ROW_GRAIN), ROW_GRAIN), :]
        dst = yl_ref.at[buf, pl.ds(pl.multiple_of(j * ROW_GRAIN, ROW_GRAIN), ROW_GRAIN), :]
        return pltpu.make_async_copy(src, dst, sem.at[buf])

    @pl.when(step == 0)
    def _():
        yl_ref[...] = jnp.zeros_like(yl_ref)

        def first(j, carry):
            piece(j, 0, src_ref[j]).start()
            return carry

        lax.fori_loop(0, np_ref[0], first, 0)

    @pl.when(step < last)
    def _():
        def ahead(j, carry):
            piece(j, 1 - slot, src_next_ref[j]).start()
            return carry

        lax.fori_loop(0, np_ref[step + 1], ahead, 0)

    def drain(j, carry):
        piece(j, slot, 0).wait()
        return carry

    lax.fori_loop(0, np_ref[step], drain, 0)

    mod = mod_ref[0]
    gt2, shf, scf = mod[5:6], mod[6:7], mod[7:8]
    meta = meta_ref[0]
    sel1, sel2 = _local_positions(meta, LOCAL_ROWS)
    wsel = jnp.where(sel1, meta[:, 2:3], jnp.where(sel2, meta[:, 3:4], 0.0))
    ffn = _dot(wsel.astype(BF16), yl_ref[slot])
    x2 = x1_ref[0] + gt2 * ffn
    r = lax.rsqrt(jnp.mean(x2 * x2, axis=-1, keepdims=True) + EPS)
    out_ref[0] = (x2 * r * nfg_ref[...]) * (1.0 + scf) + shf


def _combine(n_pieces, dst, x1, meta, modall, nfg, ys):
    bsz, seq, dm = x1.shape
    tile = TOK_TILE
    spt = seq // tile
    return pl.pallas_call(
        _combine_kernel,
        grid_spec=pltpu.PrefetchScalarGridSpec(
            num_scalar_prefetch=1,
            grid=(bsz, spt),
            in_specs=[pl.BlockSpec((LANES,), lambda b, s, npc: (b * spt + s,), memory_space=pltpu.SMEM),
                      pl.BlockSpec((LANES,), lambda b, s, npc: (jnp.minimum(b * spt + s + 1, bsz * spt - 1),),
                                   memory_space=pltpu.SMEM),
                      pl.BlockSpec((1, tile, dm), lambda b, s, npc: (b, s, 0)),
                      pl.BlockSpec((1, tile, LANES), lambda b, s, npc: (b, s, 0)),
                      pl.BlockSpec((1, 8, dm), lambda b, s, npc: (b, 0, 0)),
                      pl.BlockSpec((1, dm), lambda b, s, npc: (0, 0)),
                      pl.BlockSpec(memory_space=pl.ANY)],
            out_specs=pl.BlockSpec((1, tile, dm), lambda b, s, npc: (b, s, 0)),
            scratch_shapes=[pltpu.VMEM((2, LOCAL_ROWS, dm), BF16), pltpu.SemaphoreType.DMA((2,))]),
        out_shape=jax.ShapeDtypeStruct((bsz, seq, dm), F32),
        compiler_params=pltpu.CompilerParams(
            dimension_semantics=("arbitrary", "arbitrary"), vmem_limit_bytes=VMEM_LIMIT),
        name="combine",
    )(n_pieces, dst, dst, x1, meta, modall, nfg, ys)


def kernel(x, c, w_ada, b_ada, norm1_g, w_in, gla_w_a2, gla_b_a, gla_norm_g, hg_norm_g, hg_lb_logits,
           w_out, norm2_g, w_rg, b_rg, w_re, b_re, w_exp_gate, w_exp_up, w_exp_down, w_ada_final,
           b_ada_final, norm_f_g):
    bsz, seq, dm = x.shape
    n_tok = bsz * seq

    mod = _ada(c, w_ada[0], b_ada[0], 1536)
    mod_f = _ada(c, w_ada_final, b_ada_final, 1024)
    modall = jnp.concatenate([mod.reshape(bsz, 6, dm), mod_f.reshape(bsz, 2, dm)], axis=1)

    wi = w_in[0]
    ga0 = 2 * GLA_QK + 2 * GLA_W
    win = jnp.concatenate([wi[:, :ga0],
                           jnp.pad(wi[:, ga0:ga0 + GLA_RANK], ((0, 0), (0, LANES - GLA_RANK))),
                           wi[:, ga0 + GLA_RANK:]], axis=1).astype(BF16)
    wa2 = jnp.pad(gla_w_a2[0], ((0, LANES - GLA_RANK), (0, 0))).astype(BF16)
    pad_r = LANES - N_EXPERTS - N_GROUPS
    wr = jnp.pad(jnp.concatenate([w_re[0], w_rg[0]], axis=1), ((0, 0), (0, pad_r))).astype(BF16)
    br = jnp.pad(jnp.concatenate([b_re[0], b_rg[0]]), (0, pad_r)).reshape(1, LANES)

    assert seq % TOK_TILE == 0
    x1, h2, meta, info, total = _mixer(
        x, modall, norm1_g[0].reshape(1, dm), win, wa2, gla_b_a[0].reshape(1, GLA_QK),
        gla_norm_g[0].reshape(1, GLA_DV), hg_norm_g[0].reshape(1, HG_DV), hg_lb_logits,
        w_out[0].astype(BF16), norm2_g[0].reshape(1, dm), wr, br)

    tm = EXP_TILE
    n_tok_tiles = n_tok // TOK_TILE
    n_tiles = (2 * n_tok + n_tok_tiles * N_EXPERTS * (ROW_GRAIN - 1)) // tm + N_EXPERTS
    i32 = jnp.int32
    run = info[:, 0, :N_EXPERTS].astype(i32)
    before = info[:, 1, :N_EXPERTS].astype(i32)
    loff = info[:, 2, :N_EXPERTS].astype(i32)
    rows_e = total[0, :N_EXPERTS].astype(i32)
    region = ((rows_e + tm - 1) // tm) * tm
    ends = jnp.cumsum(region)
    tails = jnp.where(region > 0, ends - tm, -1).astype(i32)
    gbase = (ends - region)[None, :] + before
    n_pieces = (jnp.sum(run, axis=1) // ROW_GRAIN).astype(i32)
    piece_row = jnp.arange(LANES, dtype=i32) * ROW_GRAIN
    owner = jnp.sum(piece_row[None, :, None] >= (loff + run)[:, None, :], axis=-1)
    owner = jnp.minimum(owner, N_EXPERTS - 1)
    shift = gbase - loff
    pick = owner[..., None] == jnp.arange(N_EXPERTS, dtype=i32)
    dst = (jnp.sum(jnp.where(pick, shift[:, None, :], 0), axis=-1) + piece_row[None, :])
    dst = jnp.clip(dst, 0, n_tiles * tm - ROW_GRAIN).astype(i32).reshape(n_tok_tiles * LANES)
    n_used = (ends[-1] // tm).astype(i32).reshape(1)
    tile_start = jnp.arange(n_tiles, dtype=i32) * tm
    tile_start = jnp.minimum(tile_start, ends[-1] - tm)
    tile_expert = jnp.sum(tile_start[:, None] >= ends[None, :], axis=1).astype(i32)

    meta2 = meta.reshape(n_tok, LANES)
    xs = _dispatch(tails, n_pieces, dst, meta2, h2.reshape(n_tok, dm), n_tiles * tm)
    ys = _experts(tile_expert, n_used, xs, w_exp_gate[0], w_exp_up[0], w_exp_down[0], n_tiles)
    return _combine(n_pieces, dst, x1, meta, modall, norm_f_g.reshape(1, dm), ys)
```

```python
import jax
import jax.numpy as jnp
from jax import lax
from jax.experimental import pallas as pl
from jax.experimental.pallas import tpu as pltpu

F32 = jnp.float32
BF16 = jnp.bfloat16

GLA_HEADS, GLA_DK, GLA_DV, GLA_RANK, GLA_TAU = 4, 64, 128, 16, 16.0
HG_HEADS, HG_DK, HG_DV = 4, 128, 128
GLA_QK = GLA_HEADS * GLA_DK
GLA_W = GLA_HEADS * GLA_DV
HG_W = HG_HEADS * HG_DK
CHUNK = 128
N_GROUPS, EXPERTS_PER_GROUP, N_EXPERTS = 4, 8, 32
EPS = 1e-6
LANES = 128

C_GQ, C_GK, C_GV, C_GG, C_GA = 0, 256, 512, 1024, 1536
C_HQ, C_HF, C_HI, C_HG, C_END = 1664, 2176, 2688, 3200, 3712

TOK_TILE = 512
MIX_SUB = 256
EXP_TILE = 1024
ROW_GRAIN = 16
PIECE_UNROLL = 8
ZERO_BITS = (EXP_TILE // ROW_GRAIN).bit_length() - 1
LOCAL_ROWS = 2 * TOK_TILE + N_EXPERTS * ROW_GRAIN
assert LOCAL_ROWS // ROW_GRAIN <= LANES
EXP_CLAMP = 80.0
VMEM_LIMIT = 56 * 1024 * 1024


def _silu(t):
    return t * jax.nn.sigmoid(t)


def _dot(a, b):
    return jnp.dot(a, b, preferred_element_type=F32)


def _dot_nt(a, b):
    return lax.dot_general(a, b, (((1,), (1,)), ((), ())), preferred_element_type=F32)


def _dot_tn(a, b):
    return lax.dot_general(a, b, (((0,), (0,)), ((), ())), preferred_element_type=F32)


def _ada_kernel(c_ref, w_ref, b_ref, o_ref):
    ca = _silu(c_ref[...]).astype(BF16)
    o_ref[...] = _dot(ca, w_ref[...].astype(BF16)) + b_ref[...]


def _ada(c, w, b, tn):
    bsz, dm = c.shape
    n = w.shape[1]
    return pl.pallas_call(
        _ada_kernel,
        grid=(n // tn,),
        in_specs=[pl.BlockSpec((bsz, dm), lambda j: (0, 0)),
                  pl.BlockSpec((dm, tn), lambda j: (0, j)),
                  pl.BlockSpec((1, tn), lambda j: (0, j))],
        out_specs=pl.BlockSpec((bsz, tn), lambda j: (0, j)),
        out_shape=jax.ShapeDtypeStruct((bsz, n), F32),
        compiler_params=pltpu.CompilerParams(vmem_limit_bytes=VMEM_LIMIT),
        name="ada",
    )(c, w, b.reshape(1, n))


def _split2(t):
    hi = t.astype(BF16)
    lo = (t - hi.astype(F32)).astype(BF16)
    return hi, lo


def _recurrence(q, k, v, ld, st_ref, blocktril, causal, heads, dk, dv):
    t, w = q.shape
    nc = t // CHUNK
    ld_hi, ld_lo = _split2(ld)
    b = (_dot(blocktril, ld_hi) + _dot(blocktril, ld_lo)).reshape(nc, CHUNK, w)
    bm = b[:, CHUNK // 2 - 1:CHUNK // 2, :]
    bl = b[:, CHUNK - 1:CHUNK, :]
    qt = q.reshape(nc, CHUNK, w) * jnp.exp(jnp.minimum(b - bm, EXP_CLAMP))
    kt = k.reshape(nc, CHUNK, w) * jnp.exp(jnp.minimum(bm - b, EXP_CLAMP))
    qs = (qt * jnp.exp(bm)).astype(BF16).reshape(t, w)
    ks = (kt * jnp.exp(bl - bm)).astype(BF16).reshape(t, w)
    qt = qt.astype(BF16).reshape(t, w)
    kt = kt.astype(BF16).reshape(t, w)
    dec = jnp.exp(bl)
    vb = v.astype(BF16)
    states = [st_ref[h] for h in range(heads)]
    out_rows = []
    for c in range(nc):
        rs = slice(c * CHUNK, (c + 1) * CHUNK)
        outs = []
        for h in range(heads):
            ks_ = slice(h * dk, (h + 1) * dk)
            vs_ = slice(h * dv, (h + 1) * dv)
            attn = _dot_nt(qt[rs, ks_], kt[rs, ks_])
            attn = jnp.where(causal, attn, 0.0).astype(BF16)
            outs.append(_dot(attn, vb[rs, vs_]) + _dot_nt(qs[rs, ks_], states[h].astype(BF16)))
            states[h] = states[h] * dec[c][:, ks_] + _dot_tn(vb[rs, vs_], ks[rs, ks_])
        out_rows.append(jnp.concatenate(outs, axis=-1))
    for h in range(heads):
        st_ref[h] = states[h]
    return jnp.concatenate(out_rows, axis=0)


def _head_norm(o, g, heads, dv):
    outs = []
    for h in range(heads):
        oh = o[:, h * dv:(h + 1) * dv]
        r = lax.rsqrt(jnp.mean(oh * oh, axis=-1, keepdims=True) + EPS)
        outs.append(oh * r * g)
    return jnp.concatenate(outs, axis=-1)


def _mixer_kernel(x_ref, mod_ref, n1g_ref, win_ref, wa2_ref, ba_ref, glag_ref, hgg_ref, lbl_ref,
                  wout_ref, n2g_ref, wr_ref, br_ref,
                  x1_ref, h2_ref, meta_ref, info_ref, cnt_ref,
                  sg_ref, sh_ref, carry_ref, proj_ref, o_ref):
    bi = pl.program_id(0)
    si = pl.program_id(1)

    @pl.when(si == 0)
    def _():
        sg_ref[...] = jnp.zeros_like(sg_ref)
        sh_ref[...] = jnp.zeros_like(sh_ref)

    @pl.when((bi == 0) & (si == 0))
    def _():
        carry_ref[...] = jnp.zeros_like(carry_ref)

    mod = mod_ref[0]
    sh1, sc1 = mod[0:1], mod[1:2]
    for ti in range(TOK_TILE // MIX_SUB):
        rows = slice(ti * MIX_SUB, (ti + 1) * MIX_SUB)
        x = x_ref[0, rows, :]
        r1 = lax.rsqrt(jnp.mean(x * x, axis=-1, keepdims=True) + EPS)
        hb = ((x * r1 * n1g_ref[...]) * (1.0 + sc1) + sh1).astype(BF16)
        proj_ref[rows, :] = _dot(hb, win_ref[...])
        _mixer_rows(x_ref.at[0, rows, :], mod_ref, wa2_ref, ba_ref, glag_ref, hgg_ref,
                    lbl_ref, wout_ref, n2g_ref, x1_ref.at[0, rows, :], h2_ref.at[0, rows, :],
                    proj_ref.at[rows, :], o_ref.at[rows, :], sg_ref, sh_ref)
    _route(h2_ref.at[0], wr_ref, br_ref, meta_ref.at[0], info_ref.at[0], cnt_ref, carry_ref)


def _mixer_rows(x_ref, mod_ref, wa2_ref, ba_ref, glag_ref, hgg_ref, lbl_ref, wout_ref, n2g_ref,
                x1_ref, h2_ref, proj_ref, o_ref, sg_ref, sh_ref):
    tile = MIX_SUB
    mod = mod_ref[0]
    gt1, sh2, sc2 = mod[2:3], mod[3:4], mod[4:5]

    lbl = lbl_ref[...]
    lmax = jnp.max(lbl, axis=0, keepdims=True)
    lexp = jnp.exp(lbl - lmax)
    lb = lexp[0:1] / jnp.sum(lexp, axis=0, keepdims=True)

    row = lax.broadcasted_iota(jnp.int32, (CHUNK, CHUNK), 0)
    col = lax.broadcasted_iota(jnp.int32, (CHUNK, CHUNK), 1)
    causal = row >= col
    trow = lax.broadcasted_iota(jnp.int32, (tile, tile), 0)
    tcol = lax.broadcasted_iota(jnp.int32, (tile, tile), 1)
    same_chunk = (trow - tcol) <= (trow & (CHUNK - 1))
    blocktril = ((trow >= tcol) & same_chunk).astype(BF16)

    a_logit = _dot(proj_ref[:, C_GA:C_GA + LANES].astype(BF16), wa2_ref[...]) + ba_ref[...]
    log_sig = jnp.minimum(a_logit, 0.0) - jnp.log(1.0 + jnp.exp(-jnp.abs(a_logit)))
    q = proj_ref[:, C_GQ:C_GK] * (GLA_DK ** -0.5)
    o_a = _recurrence(q, proj_ref[:, C_GK:C_GV], proj_ref[:, C_GV:C_GG], log_sig * (1.0 / GLA_TAU),
                      sg_ref, blocktril, causal, GLA_HEADS, GLA_DK, GLA_DV)
    o_a = _head_norm(o_a, glag_ref[...], GLA_HEADS, GLA_DV) * _silu(proj_ref[:, C_GG:C_GA])
    o_ref[:, 0:GLA_W] = o_a.astype(BF16)
    hq = proj_ref[:, C_HQ:C_HF] * (HG_DK ** -0.5)
    sig = jax.nn.sigmoid(proj_ref[:, C_HF:C_HI])
    f = lb + (1.0 - lb) * sig
    kh = (1.0 - lb) * (1.0 - sig)
    o_b = _recurrence(hq, kh, proj_ref[:, C_HI:C_HG], jnp.log(f), sh_ref,
                      blocktril, causal, HG_HEADS, HG_DK, HG_DV)
    o_b = _head_norm(o_b, hgg_ref[...], HG_HEADS, HG_DV) * _silu(proj_ref[:, C_HG:C_END])
    o_ref[:, GLA_W:GLA_W + HG_W] = o_b.astype(BF16)

    mix = _dot(o_ref[...], wout_ref[...])
    x1 = x_ref[...] + gt1 * mix
    x1_ref[...] = x1
    r2 = lax.rsqrt(jnp.mean(x1 * x1, axis=-1, keepdims=True) + EPS)
    h2 = (x1 * r2 * n2g_ref[...]) * (1.0 + sc2) + sh2
    h2_ref[...] = h2.astype(BF16)


def _route(h2_ref, wr_ref, br_ref, meta_ref, info_ref, cnt_ref, carry_ref):
    tile = TOK_TILE
    logits = _dot(h2_ref[...], wr_ref[...]) + br_ref[...]
    lane = lax.broadcasted_iota(jnp.int32, (tile, LANES), 1)
    neg = jnp.float32(-jnp.inf)
    big = jnp.int32(1 << 20)
    gl = jnp.where((lane >= N_EXPERTS) & (lane < N_EXPERTS + N_GROUPS), logits, neg)
    gmax = jnp.max(gl, axis=-1, keepdims=True)
    gidx = jnp.min(jnp.where(gl == gmax, lane, big), axis=-1, keepdims=True) - N_EXPERTS
    g_p = 1.0 / jnp.sum(jnp.exp(gl - gmax), axis=-1, keepdims=True)
    in_group = (lane >= gidx * EXPERTS_PER_GROUP) & (lane < (gidx + 1) * EXPERTS_PER_GROUP)
    el = jnp.where(in_group, logits, neg)
    m1 = jnp.max(el, axis=-1, keepdims=True)
    i1 = jnp.min(jnp.where(el == m1, lane, big), axis=-1, keepdims=True)
    el2 = jnp.where(lane == i1, neg, el)
    m2 = jnp.max(el2, axis=-1, keepdims=True)
    i2 = jnp.min(jnp.where(el2 == m2, lane, big), axis=-1, keepdims=True)
    t = jnp.exp(m2 - m1)
    w1 = g_p / (1.0 + t)
    w2 = g_p * t / (1.0 + t)

    sel1 = lane == i1
    sel2 = lane == i2
    onehot = (sel1 | sel2).astype(BF16)
    trow = lax.broadcasted_iota(jnp.int32, (tile, tile), 0)
    tcol = lax.broadcasted_iota(jnp.int32, (tile, tile), 1)
    strict = (trow > tcol).astype(BF16)
    lcum = _dot(strict, onehot)
    cnt = jnp.sum(onehot.astype(F32), axis=0, keepdims=True)
    run = jnp.floor((cnt + (ROW_GRAIN - 1)) * (1.0 / ROW_GRAIN)) * ROW_GRAIN
    erow = lax.broadcasted_iota(jnp.int32, (LANES, LANES), 0)
    ecol = lax.broadcasted_iota(jnp.int32, (LANES, LANES), 1)
    before = (erow < ecol).astype(BF16)
    run8 = jnp.broadcast_to(run, (8, LANES))
    loff = _dot(run8.astype(BF16), before)
    pos = lcum + loff[0:1]
    p1 = jnp.sum(jnp.where(sel1, pos, 0.0), axis=-1, keepdims=True)
    p2 = jnp.sum(jnp.where(sel2, pos, 0.0), axis=-1, keepdims=True)

    srow = lax.broadcasted_iota(jnp.int32, (8, LANES), 0)
    info = jnp.where(srow == 0, run8, jnp.where(srow == 1, carry_ref[...], jnp.where(srow == 2, loff, 0.0)))
    info_ref[...] = info
    carry_ref[...] = carry_ref[...] + run
    cnt_ref[...] = carry_ref[...]

    meta = jnp.where(lane == 0, p1, 0.0)
    meta = jnp.where(lane == 1, p2, meta)
    meta = jnp.where(lane == 2, w1, meta)
    meta = jnp.where(lane == 3, w2, meta)
    meta_ref[...] = meta


def _mixer(x, modall, n1g, win, wa2, ba, glag, hgg, lbl, wout, n2g, wr, br):
    bsz, seq, dm = x.shape
    tile = TOK_TILE
    spt = seq // tile
    grid = (bsz, spt)
    const = lambda shape: pl.BlockSpec(shape, lambda b, s: (0,) * len(shape))
    tok = lambda width: pl.BlockSpec((1, tile, width), lambda b, s: (b, s, 0))
    return pl.pallas_call(
        _mixer_kernel,
        grid=grid,
        in_specs=[tok(dm),
                  pl.BlockSpec((1, 8, dm), lambda b, s: (b, 0, 0)),
                  const((1, dm)), const((dm, C_END)), const((LANES, GLA_QK)), const((1, GLA_QK)),
                  const((1, GLA_DV)), const((1, HG_DV)), const((2, HG_W)),
                  const((dm, dm)), const((1, dm)), const((dm, LANES)), const((1, LANES))],
        out_specs=[tok(dm), tok(dm), tok(LANES),
                   pl.BlockSpec((1, 8, LANES), lambda b, s: (b * spt + s, 0, 0)),
                   const((1, LANES))],
        out_shape=[jax.ShapeDtypeStruct((bsz, seq, dm), F32),
                   jax.ShapeDtypeStruct((bsz, seq, dm), BF16),
                   jax.ShapeDtypeStruct((bsz, seq, LANES), F32),
                   jax.ShapeDtypeStruct((bsz * spt, 8, LANES), F32),
                   jax.ShapeDtypeStruct((1, LANES), F32)],
        scratch_shapes=([pltpu.VMEM((GLA_HEADS, GLA_DV, GLA_DK), F32),
                         pltpu.VMEM((HG_HEADS, HG_DV, HG_DK), F32),
                         pltpu.VMEM((1, LANES), F32),
                         pltpu.VMEM((tile, C_END), F32),
                         pltpu.VMEM((tile, dm), BF16)]),
        compiler_params=pltpu.CompilerParams(
            dimension_semantics=("arbitrary", "arbitrary"), vmem_limit_bytes=VMEM_LIMIT),
        name="mixer",
    )(x, modall, n1g, win, wa2, ba, glag, hgg, lbl, wout, n2g, wr, br)


def _local_positions(meta, rows):
    lane = lax.broadcasted_iota(jnp.int32, (meta.shape[0], rows), 1)
    p1 = meta[:, 0:1].astype(jnp.int32)
    p2 = meta[:, 1:2].astype(jnp.int32)
    return lane == p1, lane == p2


def _dispatch_kernel(pad_ref, np_ref, dst_ref, meta_ref, h2_ref, xs_ref, xl_ref, zero_ref, sem, zsem):
    @pl.when(pl.program_id(0) == 0)
    def _():
        zero_ref[...] = jnp.zeros_like(zero_ref)
        total = 0
        for e in range(N_EXPERTS):
            row = pad_ref[e]
            n_grain = pad_ref[N_EXPERTS + e] // ROW_GRAIN
            total = total + pad_ref[N_EXPERTS + e]
            for bit in reversed(range(ZERO_BITS)):
                size = ROW_GRAIN << bit
                on = (n_grain >> bit) & 1

                @pl.when(on == 1)
                def _():
                    pltpu.make_async_copy(
                        zero_ref.at[pl.ds(0, size), :],
                        xs_ref.at[pl.ds(pl.multiple_of(row, ROW_GRAIN), size), :], zsem).start()

                row = row + on * size

        @pl.when(total > 0)
        def _():
            rows = pl.ds(0, pl.multiple_of(total, ROW_GRAIN))
            pltpu.make_async_copy(xs_ref.at[rows, :], xs_ref.at[rows, :], zsem).wait()

    step = pl.program_id(0)
    slot = step % 2

    sel1, sel2 = _local_positions(meta_ref[...], LOCAL_ROWS)
    xl_ref[slot] = _dot_tn((sel1 | sel2).astype(BF16), h2_ref[...]).astype(BF16)

    def piece(j, buf, dst_row):
        src = xl_ref.at[buf, pl.ds(pl.multiple_of(j * ROW_GRAIN, ROW_GRAIN), ROW_GRAIN), :]
        dst = xs_ref.at[pl.ds(pl.multiple_of(dst_row, ROW_GRAIN), ROW_GRAIN), :]
        return pltpu.make_async_copy(src, dst, sem.at[buf])

    def issue(i, carry):
        for u in range(PIECE_UNROLL):
            j = i * PIECE_UNROLL + u
            piece(j, slot, dst_ref[j]).start()
        return carry

    lax.fori_loop(0, np_ref[step] // PIECE_UNROLL, issue, 0)

    def drain(buf, n):
        rows = pl.ds(0, n * ROW_GRAIN)
        pltpu.make_async_copy(xl_ref.at[buf, rows, :], xs_ref.at[rows, :], sem.at[buf]).wait()

    @pl.when(step > 0)
    def _():
        drain(1 - slot, np_ref[step - 1])

    @pl.when(step == pl.num_programs(0) - 1)
    def _():
        drain(slot, np_ref[step])


def _dispatch(pads, n_pieces, dst, meta, h2, n_rows):
    n, dm = h2.shape
    tile = TOK_TILE
    return pl.pallas_call(
        _dispatch_kernel,
        grid_spec=pltpu.PrefetchScalarGridSpec(
            num_scalar_prefetch=2,
            grid=(n // tile,),
            in_specs=[pl.BlockSpec((LANES,), lambda i, tl, npc: (i,), memory_space=pltpu.SMEM),
                      pl.BlockSpec((tile, LANES), lambda i, tl, npc: (i, 0)),
                      pl.BlockSpec((tile, dm), lambda i, tl, npc: (i, 0))],
            out_specs=pl.BlockSpec(memory_space=pl.ANY),
            scratch_shapes=[pltpu.VMEM((2, LOCAL_ROWS, dm), BF16),
                            pltpu.VMEM((ROW_GRAIN << (ZERO_BITS - 1), dm), BF16),
                            pltpu.SemaphoreType.DMA((2,)), pltpu.SemaphoreType.DMA(())]),
        out_shape=jax.ShapeDtypeStruct((n_rows, dm), BF16),
        compiler_params=pltpu.CompilerParams(
            dimension_semantics=("arbitrary",), vmem_limit_bytes=VMEM_LIMIT),
        name="dispatch",
    )(pads, n_pieces, dst, meta, h2)


def _experts_kernel(te_ref, nu_ref, xs_ref, wg_ref, wu_ref, wd_ref, ys_ref):
    i = pl.program_id(0)

    @pl.when(i < nu_ref[0])
    def _():
        xb = xs_ref[...]
        g = _dot(xb, wg_ref[0].astype(BF16))
        u = _dot(xb, wu_ref[0].astype(BF16))
        a = (_silu(g) * u).astype(BF16)
        ys_ref[...] = _dot(a, wd_ref[0].astype(BF16)).astype(BF16)


def _experts(tile_expert, n_used, xs, wg, wu, wd, n_tiles):
    rows, dm = xs.shape
    tm = EXP_TILE
    ff = wg.shape[-1]

    def row_map(i, te, nu):
        return (jnp.minimum(i, nu[0] - 1), 0)

    def out_map(i, te, nu):
        return (jnp.where(i < nu[0], i, n_tiles), 0)

    return pl.pallas_call(
        _experts_kernel,
        grid_spec=pltpu.PrefetchScalarGridSpec(
            num_scalar_prefetch=2,
            grid=(n_tiles,),
            in_specs=[pl.BlockSpec((tm, dm), row_map),
                      pl.BlockSpec((1, dm, ff), lambda i, te, nu: (te[i], 0, 0)),
                      pl.BlockSpec((1, dm, ff), lambda i, te, nu: (te[i], 0, 0)),
                      pl.BlockSpec((1, ff, dm), lambda i, te, nu: (te[i], 0, 0))],
            out_specs=pl.BlockSpec((tm, dm), out_map)),
        out_shape=jax.ShapeDtypeStruct(((n_tiles + 1) * tm, dm), BF16),
        compiler_params=pltpu.CompilerParams(
            dimension_semantics=("arbitrary",), vmem_limit_bytes=VMEM_LIMIT),
        name="experts",
    )(tile_expert, n_used, xs, wg, wu, wd)


def _combine_kernel(np_ref, src_ref, src_next_ref, x1_ref, meta_ref, mod_ref, nfg_ref, ys_ref, out_ref,
                    yl_ref, sem):
    step = pl.program_id(0) * pl.num_programs(1) + pl.program_id(1)
    last = pl.num_programs(0) * pl.num_programs(1) - 1
    slot = step % 2

    def piece(j, buf, src_row):
        src = ys_ref.at[pl.ds(pl.multiple_of(src_row, ROW_GRAIN), ROW_GRAIN), :]
        dst = yl_ref.at[buf, pl.ds(pl.multiple_of(j * ROW_GRAIN, ROW_GRAIN), ROW_GRAIN), :]
        return pltpu.make_async_copy(src, dst, sem.at[buf])

    def fetch(rows_ref, buf, count):
        def body(i, carry):
            for u in range(PIECE_UNROLL):
                j = i * PIECE_UNROLL + u
                piece(j, buf, rows_ref[j]).start()
            return carry

        lax.fori_loop(0, count // PIECE_UNROLL, body, 0)

    @pl.when(step == 0)
    def _():
        yl_ref[...] = jnp.zeros_like(yl_ref)
        fetch(src_ref, 0, np_ref[0])

    @pl.when(step < last)
    def _():
        fetch(src_next_ref, 1 - slot, np_ref[step + 1])

    rows = pl.ds(0, np_ref[step] * ROW_GRAIN)
    pltpu.make_async_copy(ys_ref.at[rows, :], yl_ref.at[slot, rows, :], sem.at[slot]).wait()

    mod = mod_ref[0]
    gt2, shf, scf = mod[5:6], mod[6:7], mod[7:8]
    meta = meta_ref[0]
    sel1, sel2 = _local_positions(meta, LOCAL_ROWS)
    wsel = jnp.where(sel1, meta[:, 2:3], jnp.where(sel2, meta[:, 3:4], 0.0))
    ffn = _dot(wsel.astype(BF16), yl_ref[slot])
    x2 = x1_ref[0] + gt2 * ffn
    r = lax.rsqrt(jnp.mean(x2 * x2, axis=-1, keepdims=True) + EPS)
    out_ref[0] = (x2 * r * nfg_ref[...]) * (1.0 + scf) + shf


def _combine(n_pieces, dst, x1, meta, modall, nfg, ys):
    bsz, seq, dm = x1.shape
    tile = TOK_TILE
    spt = seq // tile
    return pl.pallas_call(
        _combine_kernel,
        grid_spec=pltpu.PrefetchScalarGridSpec(
            num_scalar_prefetch=1,
            grid=(bsz, spt),
            in_specs=[pl.BlockSpec((LANES,), lambda b, s, npc: (b * spt + s,), memory_space=pltpu.SMEM),
                      pl.BlockSpec((LANES,), lambda b, s, npc: (jnp.minimum(b * spt + s + 1, bsz * spt - 1),),
                                   memory_space=pltpu.SMEM),
                      pl.BlockSpec((1, tile, dm), lambda b, s, npc: (b, s, 0)),
                      pl.BlockSpec((1, tile, LANES), lambda b, s, npc: (b, s, 0)),
                      pl.BlockSpec((1, 8, dm), lambda b, s, npc: (b, 0, 0)),
                      pl.BlockSpec((1, dm), lambda b, s, npc: (0, 0)),
                      pl.BlockSpec(memory_space=pl.ANY)],
            out_specs=pl.BlockSpec((1, tile, dm), lambda b, s, npc: (b, s, 0)),
            scratch_shapes=[pltpu.VMEM((2, LOCAL_ROWS, dm), BF16), pltpu.SemaphoreType.DMA((2,))]),
        out_shape=jax.ShapeDtypeStruct((bsz, seq, dm), F32),
        compiler_params=pltpu.CompilerParams(
            dimension_semantics=("arbitrary", "arbitrary"), vmem_limit_bytes=VMEM_LIMIT),
        name="combine",
    )(n_pieces, dst, dst, x1, meta, modall, nfg, ys)


def kernel(x, c, w_ada, b_ada, norm1_g, w_in, gla_w_a2, gla_b_a, gla_norm_g, hg_norm_g, hg_lb_logits,
           w_out, norm2_g, w_rg, b_rg, w_re, b_re, w_exp_gate, w_exp_up, w_exp_down, w_ada_final,
           b_ada_final, norm_f_g):
    bsz, seq, dm = x.shape
    n_tok = bsz * seq

    mod = _ada(c, w_ada[0], b_ada[0], 1536)
    mod_f = _ada(c, w_ada_final, b_ada_final, 1024)
    modall = jnp.concatenate([mod.reshape(bsz, 6, dm), mod_f.reshape(bsz, 2, dm)], axis=1)

    wi = w_in[0]
    ga0 = 2 * GLA_QK + 2 * GLA_W
    win = jnp.concatenate([wi[:, :ga0],
                           jnp.pad(wi[:, ga0:ga0 + GLA_RANK], ((0, 0), (0, LANES - GLA_RANK))),
                           wi[:, ga0 + GLA_RANK:]], axis=1).astype(BF16)
    wa2 = jnp.pad(gla_w_a2[0], ((0, LANES - GLA_RANK), (0, 0))).astype(BF16)
    pad_r = LANES - N_EXPERTS - N_GROUPS
    wr = jnp.pad(jnp.concatenate([w_re[0], w_rg[0]], axis=1), ((0, 0), (0, pad_r))).astype(BF16)
    br = jnp.pad(jnp.concatenate([b_re[0], b_rg[0]]), (0, pad_r)).reshape(1, LANES)

    assert seq % TOK_TILE == 0
    x1, h2, meta, info, total = _mixer(
        x, modall, norm1_g[0].reshape(1, dm), win, wa2, gla_b_a[0].reshape(1, GLA_QK),
        gla_norm_g[0].reshape(1, GLA_DV), hg_norm_g[0].reshape(1, HG_DV), hg_lb_logits,
        w_out[0].astype(BF16), norm2_g[0].reshape(1, dm), wr, br)

    tm = EXP_TILE
    n_tok_tiles = n_tok // TOK_TILE
    n_tiles = (2 * n_tok + n_tok_tiles * N_EXPERTS * (ROW_GRAIN - 1)) // tm + N_EXPERTS
    i32 = jnp.int32
    run = info[:, 0, :N_EXPERTS].astype(i32)
    before = info[:, 1, :N_EXPERTS].astype(i32)
    loff = info[:, 2, :N_EXPERTS].astype(i32)
    rows_e = total[0, :N_EXPERTS].astype(i32)
    region = ((rows_e + tm - 1) // tm) * tm
    ends = jnp.cumsum(region)
    pad_rows = region - rows_e
    pads = jnp.concatenate([ends - pad_rows, pad_rows]).astype(i32)
    gbase = (ends - region)[None, :] + before
    n_pieces = (jnp.sum(run, axis=1) // ROW_GRAIN).astype(i32)
    piece_row = jnp.arange(LANES, dtype=i32) * ROW_GRAIN
    owner = jnp.sum(piece_row[None, :, None] >= (loff + run)[:, None, :], axis=-1)
    owner = jnp.minimum(owner, N_EXPERTS - 1)
    shift = gbase - loff
    pick = owner[..., None] == jnp.arange(N_EXPERTS, dtype=i32)
    dst = (jnp.sum(jnp.where(pick, shift[:, None, :], 0), axis=-1) + piece_row[None, :])
    live = jnp.arange(LANES, dtype=i32)[None, :] < n_pieces[:, None]
    n_pieces = ((n_pieces + PIECE_UNROLL - 1) // PIECE_UNROLL) * PIECE_UNROLL
    lane_slot = jnp.arange(LANES, dtype=i32) % PIECE_UNROLL
    parity = (jnp.arange(n_tok_tiles, dtype=i32) % 2) * PIECE_UNROLL
    spare = n_tiles * tm + (parity[:, None] + lane_slot[None, :]) * ROW_GRAIN
    dst_out = jnp.where(live, dst, spare).astype(i32).reshape(n_tok_tiles * LANES)
    dst_in = jnp.where(live, dst, 0).astype(i32).reshape(n_tok_tiles * LANES)
    n_used = (ends[-1] // tm).astype(i32).reshape(1)
    tile_start = jnp.arange(n_tiles, dtype=i32) * tm
    tile_start = jnp.minimum(tile_start, ends[-1] - tm)
    tile_expert = jnp.sum(tile_start[:, None] >= ends[None, :], axis=1).astype(i32)

    meta2 = meta.reshape(n_tok, LANES)
    spare_rows = 2 * PIECE_UNROLL * ROW_GRAIN
    xs = _dispatch(pads, n_pieces, dst_out, meta2, h2.reshape(n_tok, dm), n_tiles * tm + spare_rows)
    ys = _experts(tile_expert, n_used, xs, w_exp_gate[0], w_exp_up[0], w_exp_down[0], n_tiles)
    return _combine(n_pieces, dst_in, x1, meta, modall, norm_f_g.reshape(1, dm), ys)
```

```python
import functools

import jax
import jax.numpy as jnp
from jax import lax
from jax.experimental import pallas as pl
from jax.experimental.pallas import tpu as pltpu

F32 = jnp.float32
BF16 = jnp.bfloat16

GLA_HEADS, GLA_DK, GLA_DV, GLA_RANK, GLA_TAU = 4, 64, 128, 16, 16.0
HG_HEADS, HG_DK, HG_DV = 4, 128, 128
GLA_QK = GLA_HEADS * GLA_DK
GLA_W = GLA_HEADS * GLA_DV
HG_W = HG_HEADS * HG_DK
CHUNK = 128
N_GROUPS, EXPERTS_PER_GROUP, N_EXPERTS = 4, 8, 32
EPS = 1e-6
LANES = 128

C_GQ, C_GK, C_GV, C_GG, C_GA = 0, 256, 512, 1024, 1536
C_HQ, C_HF, C_HI, C_HG, C_END = 1664, 2176, 2688, 3200, 3712

TOK_TILE = 512
MIX_SUB = 256
EXP_TILE = 1024
ROW_GRAIN = 16
PIECE_UNROLL = 8
ZERO_BITS = (EXP_TILE // ROW_GRAIN).bit_length() - 1
LOCAL_ROWS = 2 * TOK_TILE + N_EXPERTS * ROW_GRAIN
assert LOCAL_ROWS // ROW_GRAIN <= LANES
ROW_EXTENTS = (LOCAL_ROWS - 256, LOCAL_ROWS)
EXP_CLAMP = 80.0
VMEM_LIMIT = 56 * 1024 * 1024


def _silu(t):
    return t * jax.nn.sigmoid(t)


def _dot(a, b):
    return jnp.dot(a, b, preferred_element_type=F32)


def _dot_nt(a, b):
    return lax.dot_general(a, b, (((1,), (1,)), ((), ())), preferred_element_type=F32)


def _dot_tn(a, b):
    return lax.dot_general(a, b, (((0,), (0,)), ((), ())), preferred_element_type=F32)


def _ada_kernel(c_ref, w_ref, b_ref, o_ref):
    ca = _silu(c_ref[...]).astype(BF16)
    o_ref[...] = _dot(ca, w_ref[...].astype(BF16)) + b_ref[...]


def _ada(c, w, b, tn):
    bsz, dm = c.shape
    n = w.shape[1]
    return pl.pallas_call(
        _ada_kernel,
        grid=(n // tn,),
        in_specs=[pl.BlockSpec((bsz, dm), lambda j: (0, 0)),
                  pl.BlockSpec((dm, tn), lambda j: (0, j)),
                  pl.BlockSpec((1, tn), lambda j: (0, j))],
        out_specs=pl.BlockSpec((bsz, tn), lambda j: (0, j)),
        out_shape=jax.ShapeDtypeStruct((bsz, n), F32),
        compiler_params=pltpu.CompilerParams(vmem_limit_bytes=VMEM_LIMIT),
        name="ada",
    )(c, w, b.reshape(1, n))


def _split2(t):
    hi = t.astype(BF16)
    lo = (t - hi.astype(F32)).astype(BF16)
    return hi, lo


def _recurrence(q, k, v, ld, st_ref, blocktril, causal, heads, dk, dv):
    t, w = q.shape
    nc = t // CHUNK
    ld_hi, ld_lo = _split2(ld)
    b = (_dot(blocktril, ld_hi) + _dot(blocktril, ld_lo)).reshape(nc, CHUNK, w)
    bm = b[:, CHUNK // 2 - 1:CHUNK // 2, :]
    bl = b[:, CHUNK - 1:CHUNK, :]
    qt = q.reshape(nc, CHUNK, w) * jnp.exp(jnp.minimum(b - bm, EXP_CLAMP))
    kt = k.reshape(nc, CHUNK, w) * jnp.exp(jnp.minimum(bm - b, EXP_CLAMP))
    qs = (qt * jnp.exp(bm)).astype(BF16).reshape(t, w)
    ks = (kt * jnp.exp(bl - bm)).astype(BF16).reshape(t, w)
    qt = qt.astype(BF16).reshape(t, w)
    kt = kt.astype(BF16).reshape(t, w)
    dec = jnp.exp(bl)
    vb = v.astype(BF16)
    states = [st_ref[h] for h in range(heads)]
    out_rows = []
    for c in range(nc):
        rs = slice(c * CHUNK, (c + 1) * CHUNK)
        outs = []
        for h in range(heads):
            ks_ = slice(h * dk, (h + 1) * dk)
            vs_ = slice(h * dv, (h + 1) * dv)
            attn = _dot_nt(qt[rs, ks_], kt[rs, ks_])
            attn = jnp.where(causal, attn, 0.0).astype(BF16)
            outs.append(_dot(attn, vb[rs, vs_]) + _dot_nt(qs[rs, ks_], states[h].astype(BF16)))
            states[h] = states[h] * dec[c][:, ks_] + _dot_tn(vb[rs, vs_], ks[rs, ks_])
        out_rows.append(jnp.concatenate(outs, axis=-1))
    for h in range(heads):
        st_ref[h] = states[h]
    return jnp.concatenate(out_rows, axis=0)


def _head_norm(o, g, heads, dv):
    outs = []
    for h in range(heads):
        oh = o[:, h * dv:(h + 1) * dv]
        r = lax.rsqrt(jnp.mean(oh * oh, axis=-1, keepdims=True) + EPS)
        outs.append(oh * r * g)
    return jnp.concatenate(outs, axis=-1)


def _mixer_kernel(x_ref, mod_ref, n1g_ref, win_ref, wa2_ref, ba_ref, glag_ref, hgg_ref, lbl_ref,
                  wout_ref, n2g_ref, wr_ref, br_ref,
                  x1_ref, h2_ref, meta_ref, info_ref, cnt_ref,
                  sg_ref, sh_ref, carry_ref, proj_ref, o_ref):
    bi = pl.program_id(0)
    si = pl.program_id(1)

    @pl.when(si == 0)
    def _():
        sg_ref[...] = jnp.zeros_like(sg_ref)
        sh_ref[...] = jnp.zeros_like(sh_ref)

    @pl.when((bi == 0) & (si == 0))
    def _():
        carry_ref[...] = jnp.zeros_like(carry_ref)

    mod = mod_ref[0]
    sh1 = mod[0:1]
    gain1 = n1g_ref[...] * (1.0 + mod[1:2])
    for ti in range(TOK_TILE // MIX_SUB):
        rows = slice(ti * MIX_SUB, (ti + 1) * MIX_SUB)
        x = x_ref[0, rows, :]
        r1 = lax.rsqrt(jnp.mean(x * x, axis=-1, keepdims=True) + EPS)
        hb = (x * r1 * gain1 + sh1).astype(BF16)
        proj_ref[rows, :] = _dot(hb, win_ref[...])
        _mixer_rows(x_ref.at[0, rows, :], mod_ref, wa2_ref, ba_ref, glag_ref, hgg_ref,
                    lbl_ref, wout_ref, n2g_ref, x1_ref.at[0, rows, :], h2_ref.at[0, rows, :],
                    proj_ref.at[rows, :], o_ref.at[rows, :], sg_ref, sh_ref)
    _route(h2_ref.at[0], wr_ref, br_ref, meta_ref.at[0], info_ref.at[0], cnt_ref, carry_ref)


def _mixer_rows(x_ref, mod_ref, wa2_ref, ba_ref, glag_ref, hgg_ref, lbl_ref, wout_ref, n2g_ref,
                x1_ref, h2_ref, proj_ref, o_ref, sg_ref, sh_ref):
    tile = MIX_SUB
    mod = mod_ref[0]
    gt1, sh2, sc2 = mod[2:3], mod[3:4], mod[4:5]

    lbl = lbl_ref[...]
    lmax = jnp.max(lbl, axis=0, keepdims=True)
    lexp = jnp.exp(lbl - lmax)
    lb = lexp[0:1] / jnp.sum(lexp, axis=0, keepdims=True)

    row = lax.broadcasted_iota(jnp.int32, (CHUNK, CHUNK), 0)
    col = lax.broadcasted_iota(jnp.int32, (CHUNK, CHUNK), 1)
    causal = row >= col
    trow = lax.broadcasted_iota(jnp.int32, (tile, tile), 0)
    tcol = lax.broadcasted_iota(jnp.int32, (tile, tile), 1)
    same_chunk = (trow - tcol) <= (trow & (CHUNK - 1))
    blocktril = ((trow >= tcol) & same_chunk).astype(BF16)

    a_logit = _dot(proj_ref[:, C_GA:C_GA + LANES].astype(BF16), wa2_ref[...]) + ba_ref[...]
    log_sig = jnp.minimum(a_logit, 0.0) - jnp.log(1.0 + jnp.exp(-jnp.abs(a_logit)))
    q = proj_ref[:, C_GQ:C_GK] * (GLA_DK ** -0.5)
    o_a = _recurrence(q, proj_ref[:, C_GK:C_GV], proj_ref[:, C_GV:C_GG], log_sig * (1.0 / GLA_TAU),
                      sg_ref, blocktril, causal, GLA_HEADS, GLA_DK, GLA_DV)
    o_a = _head_norm(o_a, glag_ref[...], GLA_HEADS, GLA_DV) * _silu(proj_ref[:, C_GG:C_GA])
    o_ref[:, 0:GLA_W] = o_a.astype(BF16)
    hq = proj_ref[:, C_HQ:C_HF] * (HG_DK ** -0.5)
    sig = jax.nn.sigmoid(proj_ref[:, C_HF:C_HI])
    f = lb + (1.0 - lb) * sig
    kh = (1.0 - lb) * (1.0 - sig)
    o_b = _recurrence(hq, kh, proj_ref[:, C_HI:C_HG], jnp.log(f), sh_ref,
                      blocktril, causal, HG_HEADS, HG_DK, HG_DV)
    o_b = _head_norm(o_b, hgg_ref[...], HG_HEADS, HG_DV) * _silu(proj_ref[:, C_HG:C_END])
    o_ref[:, GLA_W:GLA_W + HG_W] = o_b.astype(BF16)

    mix = _dot(o_ref[...], wout_ref[...])
    x1 = x_ref[...] + gt1 * mix
    x1_ref[...] = x1
    r2 = lax.rsqrt(jnp.mean(x1 * x1, axis=-1, keepdims=True) + EPS)
    h2 = x1 * r2 * (n2g_ref[...] * (1.0 + sc2)) + sh2
    h2_ref[...] = h2.astype(BF16)


def _route(h2_ref, wr_ref, br_ref, meta_ref, info_ref, cnt_ref, carry_ref):
    tile = TOK_TILE
    logits = _dot(h2_ref[...], wr_ref[...]) + br_ref[...]
    lane = lax.broadcasted_iota(jnp.int32, (tile, LANES), 1)
    neg = jnp.float32(-jnp.inf)
    big = jnp.int32(1 << 20)
    gl = jnp.where((lane >= N_EXPERTS) & (lane < N_EXPERTS + N_GROUPS), logits, neg)
    gmax = jnp.max(gl, axis=-1, keepdims=True)
    gidx = jnp.min(jnp.where(gl == gmax, lane, big), axis=-1, keepdims=True) - N_EXPERTS
    g_p = 1.0 / jnp.sum(jnp.exp(gl - gmax), axis=-1, keepdims=True)
    in_group = (lane >= gidx * EXPERTS_PER_GROUP) & (lane < (gidx + 1) * EXPERTS_PER_GROUP)
    el = jnp.where(in_group, logits, neg)
    m1 = jnp.max(el, axis=-1, keepdims=True)
    i1 = jnp.min(jnp.where(el == m1, lane, big), axis=-1, keepdims=True)
    el2 = jnp.where(lane == i1, neg, el)
    m2 = jnp.max(el2, axis=-1, keepdims=True)
    i2 = jnp.min(jnp.where(el2 == m2, lane, big), axis=-1, keepdims=True)
    t = jnp.exp(m2 - m1)
    w1 = g_p / (1.0 + t)
    w2 = g_p * t / (1.0 + t)

    sel1 = lane == i1
    sel2 = lane == i2
    onehot = (sel1 | sel2).astype(BF16)
    trow = lax.broadcasted_iota(jnp.int32, (tile, tile), 0)
    tcol = lax.broadcasted_iota(jnp.int32, (tile, tile), 1)
    strict = (trow > tcol).astype(BF16)
    lcum = _dot(strict, onehot)
    cnt = jnp.sum(onehot.astype(F32), axis=0, keepdims=True)
    run = jnp.floor((cnt + (ROW_GRAIN - 1)) * (1.0 / ROW_GRAIN)) * ROW_GRAIN
    erow = lax.broadcasted_iota(jnp.int32, (LANES, LANES), 0)
    ecol = lax.broadcasted_iota(jnp.int32, (LANES, LANES), 1)
    before = (erow < ecol).astype(BF16)
    run8 = jnp.broadcast_to(run, (8, LANES))
    loff = _dot(run8.astype(BF16), before)
    pos = lcum + loff[0:1]
    p1 = jnp.sum(jnp.where(sel1, pos, 0.0), axis=-1, keepdims=True)
    p2 = jnp.sum(jnp.where(sel2, pos, 0.0), axis=-1, keepdims=True)

    srow = lax.broadcasted_iota(jnp.int32, (8, LANES), 0)
    info = jnp.where(srow == 0, run8, jnp.where(srow == 1, carry_ref[...], jnp.where(srow == 2, loff, 0.0)))
    info_ref[...] = info
    carry_ref[...] = carry_ref[...] + run
    cnt_ref[...] = carry_ref[...]

    meta = jnp.where(lane == 0, p1, 0.0)
    meta = jnp.where(lane == 1, p2, meta)
    meta = jnp.where(lane == 2, w1, meta)
    meta = jnp.where(lane == 3, w2, meta)
    meta_ref[...] = meta


def _mixer(x, modall, n1g, win, wa2, ba, glag, hgg, lbl, wout, n2g, wr, br):
    bsz, seq, dm = x.shape
    tile = TOK_TILE
    spt = seq // tile
    grid = (bsz, spt)
    const = lambda shape: pl.BlockSpec(shape, lambda b, s: (0,) * len(shape))
    tok = lambda width: pl.BlockSpec((1, tile, width), lambda b, s: (b, s, 0))
    return pl.pallas_call(
        _mixer_kernel,
        grid=grid,
        in_specs=[tok(dm),
                  pl.BlockSpec((1, 8, dm), lambda b, s: (b, 0, 0)),
                  const((1, dm)), const((dm, C_END)), const((LANES, GLA_QK)), const((1, GLA_QK)),
                  const((1, GLA_DV)), const((1, HG_DV)), const((2, HG_W)),
                  const((dm, dm)), const((1, dm)), const((dm, LANES)), const((1, LANES))],
        out_specs=[tok(dm), tok(dm), tok(LANES),
                   pl.BlockSpec((1, 8, LANES), lambda b, s: (b * spt + s, 0, 0)),
                   const((1, LANES))],
        out_shape=[jax.ShapeDtypeStruct((bsz, seq, dm), F32),
                   jax.ShapeDtypeStruct((bsz, seq, dm), BF16),
                   jax.ShapeDtypeStruct((bsz, seq, LANES), F32),
                   jax.ShapeDtypeStruct((bsz * spt, 8, LANES), F32),
                   jax.ShapeDtypeStruct((1, LANES), F32)],
        scratch_shapes=([pltpu.VMEM((GLA_HEADS, GLA_DV, GLA_DK), F32),
                         pltpu.VMEM((HG_HEADS, HG_DV, HG_DK), F32),
                         pltpu.VMEM((1, LANES), F32),
                         pltpu.VMEM((tile, C_END), F32),
                         pltpu.VMEM((tile, dm), BF16)]),
        compiler_params=pltpu.CompilerParams(
            dimension_semantics=("arbitrary", "arbitrary"), vmem_limit_bytes=VMEM_LIMIT),
        name="mixer",
    )(x, modall, n1g, win, wa2, ba, glag, hgg, lbl, wout, n2g, wr, br)


def _by_row_count(used_rows, body):
    lo = 0
    for rows in ROW_EXTENTS:
        hi = rows
        pl.when((used_rows > lo) & (used_rows <= hi))(functools.partial(body, rows))
        lo = hi


def _local_positions(meta, rows):
    lane = lax.broadcasted_iota(jnp.int32, (meta.shape[0], rows), 1)
    p1 = meta[:, 0:1].astype(jnp.int32)
    p2 = meta[:, 1:2].astype(jnp.int32)
    return lane == p1, lane == p2


def _dispatch_kernel(pad_ref, np_ref, dst_ref, meta_ref, h2_ref, xs_ref, xl_ref, zero_ref, sem, zsem):
    @pl.when(pl.program_id(0) == 0)
    def _():
        zero_ref[...] = jnp.zeros_like(zero_ref)
        total = 0
        for e in range(N_EXPERTS):
            row = pad_ref[e]
            n_grain = pad_ref[N_EXPERTS + e] // ROW_GRAIN
            total = total + pad_ref[N_EXPERTS + e]
            for bit in reversed(range(ZERO_BITS)):
                size = ROW_GRAIN << bit
                on = (n_grain >> bit) & 1

                @pl.when(on == 1)
                def _():
                    pltpu.make_async_copy(
                        zero_ref.at[pl.ds(0, size), :],
                        xs_ref.at[pl.ds(pl.multiple_of(row, ROW_GRAIN), size), :], zsem).start()

                row = row + on * size

        @pl.when(total > 0)
        def _():
            rows = pl.ds(0, pl.multiple_of(total, ROW_GRAIN))
            pltpu.make_async_copy(xs_ref.at[rows, :], xs_ref.at[rows, :], zsem).wait()

    step = pl.program_id(0)
    slot = step % 2

    def sort_rows(rows):
        sel1, sel2 = _local_positions(meta_ref[...], rows)
        xl_ref[slot, 0:rows, :] = _dot_tn((sel1 | sel2).astype(BF16), h2_ref[...]).astype(BF16)

    _by_row_count(np_ref[step] * ROW_GRAIN, sort_rows)

    def piece(j, buf, dst_row):
        src = xl_ref.at[buf, pl.ds(pl.multiple_of(j * ROW_GRAIN, ROW_GRAIN), ROW_GRAIN), :]
        dst = xs_ref.at[pl.ds(pl.multiple_of(dst_row, ROW_GRAIN), ROW_GRAIN), :]
        return pltpu.make_async_copy(src, dst, sem.at[buf])

    def issue(i, carry):
        for u in range(PIECE_UNROLL):
            j = i * PIECE_UNROLL + u
            piece(j, slot, dst_ref[j]).start()
        return carry

    lax.fori_loop(0, np_ref[step] // PIECE_UNROLL, issue, 0)

    def drain(buf, n):
        rows = pl.ds(0, n * ROW_GRAIN)
        pltpu.make_async_copy(xl_ref.at[buf, rows, :], xs_ref.at[rows, :], sem.at[buf]).wait()

    @pl.when(step > 0)
    def _():
        drain(1 - slot, np_ref[step - 1])

    @pl.when(step == pl.num_programs(0) - 1)
    def _():
        drain(slot, np_ref[step])


def _dispatch(pads, n_pieces, dst, meta, h2, n_rows):
    n, dm = h2.shape
    tile = TOK_TILE
    return pl.pallas_call(
        _dispatch_kernel,
        grid_spec=pltpu.PrefetchScalarGridSpec(
            num_scalar_prefetch=2,
            grid=(n // tile,),
            in_specs=[pl.BlockSpec((LANES,), lambda i, tl, npc: (i,), memory_space=pltpu.SMEM),
                      pl.BlockSpec((tile, LANES), lambda i, tl, npc: (i, 0)),
                      pl.BlockSpec((tile, dm), lambda i, tl, npc: (i, 0))],
            out_specs=pl.BlockSpec(memory_space=pl.ANY),
            scratch_shapes=[pltpu.VMEM((2, LOCAL_ROWS, dm), BF16),
                            pltpu.VMEM((ROW_GRAIN << (ZERO_BITS - 1), dm), BF16),
                            pltpu.SemaphoreType.DMA((2,)), pltpu.SemaphoreType.DMA(())]),
        out_shape=jax.ShapeDtypeStruct((n_rows, dm), BF16),
        compiler_params=pltpu.CompilerParams(
            dimension_semantics=("arbitrary",), vmem_limit_bytes=VMEM_LIMIT),
        name="dispatch",
    )(pads, n_pieces, dst, meta, h2)


def _experts_kernel(te_ref, nu_ref, xs_ref, wg_ref, wu_ref, wd_ref, ys_ref):
    i = pl.program_id(0)

    @pl.when(i < nu_ref[0])
    def _():
        xb = xs_ref[...]
        g = _dot(xb, wg_ref[0].astype(BF16))
        u = _dot(xb, wu_ref[0].astype(BF16))
        a = (_silu(g) * u).astype(BF16)
        ys_ref[...] = _dot(a, wd_ref[0].astype(BF16)).astype(BF16)


def _experts(tile_expert, n_used, xs, wg, wu, wd, n_tiles):
    rows, dm = xs.shape
    tm = EXP_TILE
    ff = wg.shape[-1]

    def row_map(i, te, nu):
        return (jnp.minimum(i, nu[0] - 1), 0)

    def out_map(i, te, nu):
        return (jnp.where(i < nu[0], i, n_tiles), 0)

    return pl.pallas_call(
        _experts_kernel,
        grid_spec=pltpu.PrefetchScalarGridSpec(
            num_scalar_prefetch=2,
            grid=(n_tiles,),
            in_specs=[pl.BlockSpec((tm, dm), row_map),
                      pl.BlockSpec((1, dm, ff), lambda i, te, nu: (te[i], 0, 0)),
                      pl.BlockSpec((1, dm, ff), lambda i, te, nu: (te[i], 0, 0)),
                      pl.BlockSpec((1, ff, dm), lambda i, te, nu: (te[i], 0, 0))],
            out_specs=pl.BlockSpec((tm, dm), out_map)),
        out_shape=jax.ShapeDtypeStruct(((n_tiles + 1) * tm, dm), BF16),
        compiler_params=pltpu.CompilerParams(
            dimension_semantics=("arbitrary",), vmem_limit_bytes=VMEM_LIMIT),
        name="experts",
    )(tile_expert, n_used, xs, wg, wu, wd)


def _combine_kernel(np_ref, src_ref, src_next_ref, x1_ref, meta_ref, mod_ref, nfg_ref, ys_ref, out_ref,
                    yl_ref, sem):
    step = pl.program_id(0) * pl.num_programs(1) + pl.program_id(1)
    last = pl.num_programs(0) * pl.num_programs(1) - 1
    slot = step % 2

    def piece(j, buf, src_row):
        src = ys_ref.at[pl.ds(pl.multiple_of(src_row, ROW_GRAIN), ROW_GRAIN), :]
        dst = yl_ref.at[buf, pl.ds(pl.multiple_of(j * ROW_GRAIN, ROW_GRAIN), ROW_GRAIN), :]
        return pltpu.make_async_copy(src, dst, sem.at[buf])

    def fetch(rows_ref, buf, count):
        def body(i, carry):
            for u in range(PIECE_UNROLL):
                j = i * PIECE_UNROLL + u
                piece(j, buf, rows_ref[j]).start()
            return carry

        lax.fori_loop(0, count // PIECE_UNROLL, body, 0)

    @pl.when(step == 0)
    def _():
        yl_ref[...] = jnp.zeros_like(yl_ref)
        fetch(src_ref, 0, np_ref[0])

    @pl.when(step < last)
    def _():
        fetch(src_next_ref, 1 - slot, np_ref[step + 1])

    rows = pl.ds(0, np_ref[step] * ROW_GRAIN)
    pltpu.make_async_copy(ys_ref.at[rows, :], yl_ref.at[slot, rows, :], sem.at[slot]).wait()

    mod = mod_ref[0]
    gt2, shf, scf = mod[5:6], mod[6:7], mod[7:8]
    meta = meta_ref[0]

    def combine_rows(rows):
        sel1, sel2 = _local_positions(meta, rows)
        wsel = jnp.where(sel1, meta[:, 2:3], jnp.where(sel2, meta[:, 3:4], 0.0))
        ffn = _dot(wsel.astype(BF16), yl_ref[slot, 0:rows, :])
        x2 = x1_ref[0] + gt2 * ffn
        r = lax.rsqrt(jnp.mean(x2 * x2, axis=-1, keepdims=True) + EPS)
        out_ref[0] = x2 * r * (nfg_ref[...] * (1.0 + scf)) + shf

    _by_row_count(np_ref[step] * ROW_GRAIN, combine_rows)


def _combine(n_pieces, dst, x1, meta, modall, nfg, ys):
    bsz, seq, dm = x1.shape
    tile = TOK_TILE
    spt = seq // tile
    return pl.pallas_call(
        _combine_kernel,
        grid_spec=pltpu.PrefetchScalarGridSpec(
            num_scalar_prefetch=1,
            grid=(bsz, spt),
            in_specs=[pl.BlockSpec((LANES,), lambda b, s, npc: (b * spt + s,), memory_space=pltpu.SMEM),
                      pl.BlockSpec((LANES,), lambda b, s, npc: (jnp.minimum(b * spt + s + 1, bsz * spt - 1),),
                                   memory_space=pltpu.SMEM),
                      pl.BlockSpec((1, tile, dm), lambda b, s, npc: (b, s, 0)),
                      pl.BlockSpec((1, tile, LANES), lambda b, s, npc: (b, s, 0)),
                      pl.BlockSpec((1, 8, dm), lambda b, s, npc: (b, 0, 0)),
                      pl.BlockSpec((1, dm), lambda b, s, npc: (0, 0)),
                      pl.BlockSpec(memory_space=pl.ANY)],
            out_specs=pl.BlockSpec((1, tile, dm), lambda b, s, npc: (b, s, 0)),
            scratch_shapes=[pltpu.VMEM((2, LOCAL_ROWS, dm), BF16), pltpu.SemaphoreType.DMA((2,))]),
        out_shape=jax.ShapeDtypeStruct((bsz, seq, dm), F32),
        compiler_params=pltpu.CompilerParams(
            dimension_semantics=("arbitrary", "arbitrary"), vmem_limit_bytes=VMEM_LIMIT),
        name="combine",
    )(n_pieces, dst, dst, x1, meta, modall, nfg, ys)


def kernel(x, c, w_ada, b_ada, norm1_g, w_in, gla_w_a2, gla_b_a, gla_norm_g, hg_norm_g, hg_lb_logits,
           w_out, norm2_g, w_rg, b_rg, w_re, b_re, w_exp_gate, w_exp_up, w_exp_down, w_ada_final,
           b_ada_final, norm_f_g):
    bsz, seq, dm = x.shape
    n_tok = bsz * seq

    mod = _ada(c, w_ada[0], b_ada[0], 1536)
    mod_f = _ada(c, w_ada_final, b_ada_final, 1024)
    modall = jnp.concatenate([mod.reshape(bsz, 6, dm), mod_f.reshape(bsz, 2, dm)], axis=1)

    wi = w_in[0]
    ga0 = 2 * GLA_QK + 2 * GLA_W
    win = jnp.concatenate([wi[:, :ga0],
                           jnp.pad(wi[:, ga0:ga0 + GLA_RANK], ((0, 0), (0, LANES - GLA_RANK))),
                           wi[:, ga0 + GLA_RANK:]], axis=1).astype(BF16)
    wa2 = jnp.pad(gla_w_a2[0], ((0, LANES - GLA_RANK), (0, 0))).astype(BF16)
    pad_r = LANES - N_EXPERTS - N_GROUPS
    wr = jnp.pad(jnp.concatenate([w_re[0], w_rg[0]], axis=1), ((0, 0), (0, pad_r))).astype(BF16)
    br = jnp.pad(jnp.concatenate([b_re[0], b_rg[0]]), (0, pad_r)).reshape(1, LANES)

    assert seq % TOK_TILE == 0
    x1, h2, meta, info, total = _mixer(
        x, modall, norm1_g[0].reshape(1, dm), win, wa2, gla_b_a[0].reshape(1, GLA_QK),
        gla_norm_g[0].reshape(1, GLA_DV), hg_norm_g[0].reshape(1, HG_DV), hg_lb_logits,
        w_out[0].astype(BF16), norm2_g[0].reshape(1, dm), wr, br)

    tm = EXP_TILE
    n_tok_tiles = n_tok // TOK_TILE
    n_tiles = (2 * n_tok + n_tok_tiles * N_EXPERTS * (ROW_GRAIN - 1)) // tm + N_EXPERTS
    i32 = jnp.int32
    run = info[:, 0, :N_EXPERTS].astype(i32)
    before = info[:, 1, :N_EXPERTS].astype(i32)
    loff = info[:, 2, :N_EXPERTS].astype(i32)
    rows_e = total[0, :N_EXPERTS].astype(i32)
    region = ((rows_e + tm - 1) // tm) * tm
    ends = jnp.cumsum(region)
    pad_rows = region - rows_e
    pads = jnp.concatenate([ends - pad_rows, pad_rows]).astype(i32)
    gbase = (ends - region)[None, :] + before
    n_pieces = (jnp.sum(run, axis=1) // ROW_GRAIN).astype(i32)
    piece_row = jnp.arange(LANES, dtype=i32) * ROW_GRAIN
    owner = jnp.sum(piece_row[None, :, None] >= (loff + run)[:, None, :], axis=-1)
    owner = jnp.minimum(owner, N_EXPERTS - 1)
    shift = gbase - loff
    pick = owner[..., None] == jnp.arange(N_EXPERTS, dtype=i32)
    dst = (jnp.sum(jnp.where(pick, shift[:, None, :], 0), axis=-1) + piece_row[None, :])
    live = jnp.arange(LANES, dtype=i32)[None, :] < n_pieces[:, None]
    n_pieces = ((n_pieces + PIECE_UNROLL - 1) // PIECE_UNROLL) * PIECE_UNROLL
    lane_slot = jnp.arange(LANES, dtype=i32) % PIECE_UNROLL
    parity = (jnp.arange(n_tok_tiles, dtype=i32) % 2) * PIECE_UNROLL
    spare = n_tiles * tm + (parity[:, None] + lane_slot[None, :]) * ROW_GRAIN
    dst_out = jnp.where(live, dst, spare).astype(i32).reshape(n_tok_tiles * LANES)
    dst_in = jnp.where(live, dst, 0).astype(i32).reshape(n_tok_tiles * LANES)
    n_used = (ends[-1] // tm).astype(i32).reshape(1)
    tile_start = jnp.arange(n_tiles, dtype=i32) * tm
    tile_start = jnp.minimum(tile_start, ends[-1] - tm)
    tile_expert = jnp.sum(tile_start[:, None] >= ends[None, :], axis=1).astype(i32)

    meta2 = meta.reshape(n_tok, LANES)
    spare_rows = 2 * PIECE_UNROLL * ROW_GRAIN
    xs = _dispatch(pads, n_pieces, dst_out, meta2, h2.reshape(n_tok, dm), n_tiles * tm + spare_rows)
    ys = _experts(tile_expert, n_used, xs, w_exp_gate[0], w_exp_up[0], w_exp_down[0], n_tiles)
    return _combine(n_pieces, dst_in, x1, meta, modall, norm_f_g.reshape(1, dm), ys)
```

```python
import functools

import jax
import jax.numpy as jnp
from jax import lax
from jax.experimental import pallas as pl
from jax.experimental.pallas import tpu as pltpu

F32 = jnp.float32
BF16 = jnp.bfloat16

GLA_HEADS, GLA_DK, GLA_DV, GLA_RANK, GLA_TAU = 4, 64, 128, 16, 16.0
HG_HEADS, HG_DK, HG_DV = 4, 128, 128
GLA_QK = GLA_HEADS * GLA_DK
GLA_W = GLA_HEADS * GLA_DV
HG_W = HG_HEADS * HG_DK
CHUNK = 128
N_GROUPS, EXPERTS_PER_GROUP, N_EXPERTS = 4, 8, 32
EPS = 1e-6
LANES = 128

C_GQ, C_GK, C_GV, C_GG, C_GA = 0, 256, 512, 1024, 1536
C_HQ, C_HF, C_HI, C_HG, C_END = 1664, 2176, 2688, 3200, 3712

TOK_TILE = 512
MIX_SUB = 256
EXP_TILE = 1024
ROW_GRAIN = 16
PIECE_UNROLL = 8
ZERO_BITS = (EXP_TILE // ROW_GRAIN).bit_length() - 1
LOCAL_ROWS = 2 * TOK_TILE + N_EXPERTS * ROW_GRAIN
assert LOCAL_ROWS // ROW_GRAIN <= LANES
ROW_EXTENTS = (LOCAL_ROWS - 256, LOCAL_ROWS)
EXPERT_EXTENTS = (EXP_TILE // 4, EXP_TILE // 2, 3 * EXP_TILE // 4, EXP_TILE)
EXP_CLAMP = 80.0
VMEM_LIMIT = 56 * 1024 * 1024


def _silu(t):
    return t * jax.nn.sigmoid(t)


def _dot(a, b):
    return jnp.dot(a, b, preferred_element_type=F32)


def _dot_nt(a, b):
    return lax.dot_general(a, b, (((1,), (1,)), ((), ())), preferred_element_type=F32)


def _dot_tn(a, b):
    return lax.dot_general(a, b, (((0,), (0,)), ((), ())), preferred_element_type=F32)


def _ada_kernel(c_ref, w_ref, b_ref, o_ref):
    ca = _silu(c_ref[...]).astype(BF16)
    o_ref[...] = _dot(ca, w_ref[...].astype(BF16)) + b_ref[...]


def _ada(c, w, b, tn):
    bsz, dm = c.shape
    n = w.shape[1]
    return pl.pallas_call(
        _ada_kernel,
        grid=(n // tn,),
        in_specs=[pl.BlockSpec((bsz, dm), lambda j: (0, 0)),
                  pl.BlockSpec((dm, tn), lambda j: (0, j)),
                  pl.BlockSpec((1, tn), lambda j: (0, j))],
        out_specs=pl.BlockSpec((bsz, tn), lambda j: (0, j)),
        out_shape=jax.ShapeDtypeStruct((bsz, n), F32),
        compiler_params=pltpu.CompilerParams(vmem_limit_bytes=VMEM_LIMIT),
        name="ada",
    )(c, w, b.reshape(1, n))


def _split2(t):
    hi = t.astype(BF16)
    lo = (t - hi.astype(F32)).astype(BF16)
    return hi, lo


def _recurrence(q, k, v, ld, st_ref, blocktril, causal, heads, dk, dv):
    t, w = q.shape
    nc = t // CHUNK
    ld_hi, ld_lo = _split2(ld)
    b = (_dot(blocktril, ld_hi) + _dot(blocktril, ld_lo)).reshape(nc, CHUNK, w)
    bm = b[:, CHUNK // 2 - 1:CHUNK // 2, :]
    bl = b[:, CHUNK - 1:CHUNK, :]
    qt = q.reshape(nc, CHUNK, w) * jnp.exp(jnp.minimum(b - bm, EXP_CLAMP))
    kt = k.reshape(nc, CHUNK, w) * jnp.exp(jnp.minimum(bm - b, EXP_CLAMP))
    qs = (qt * jnp.exp(bm)).astype(BF16).reshape(t, w)
    ks = (kt * jnp.exp(bl - bm)).astype(BF16).reshape(t, w)
    qt = qt.astype(BF16).reshape(t, w)
    kt = kt.astype(BF16).reshape(t, w)
    dec = jnp.exp(bl)
    vb = v.astype(BF16)
    states = [st_ref[h] for h in range(heads)]
    out_rows = []
    for c in range(nc):
        rs = slice(c * CHUNK, (c + 1) * CHUNK)
        outs = []
        for h in range(heads):
            ks_ = slice(h * dk, (h + 1) * dk)
            vs_ = slice(h * dv, (h + 1) * dv)
            attn = _dot_nt(qt[rs, ks_], kt[rs, ks_])
            attn = jnp.where(causal, attn, 0.0).astype(BF16)
            outs.append(_dot(attn, vb[rs, vs_]) + _dot_nt(qs[rs, ks_], states[h].astype(BF16)))
            states[h] = states[h] * dec[c][:, ks_] + _dot_tn(vb[rs, vs_], ks[rs, ks_])
        out_rows.append(jnp.concatenate(outs, axis=-1))
    for h in range(heads):
        st_ref[h] = states[h]
    return jnp.concatenate(out_rows, axis=0)


def _head_norm(o, g, heads, dv):
    outs = []
    for h in range(heads):
        oh = o[:, h * dv:(h + 1) * dv]
        r = lax.rsqrt(jnp.mean(oh * oh, axis=-1, keepdims=True) + EPS)
        outs.append(oh * r * g)
    return jnp.concatenate(outs, axis=-1)


def _mixer_kernel(x_ref, mod_ref, n1g_ref, win_ref, wa2_ref, ba_ref, glag_ref, hgg_ref, lbl_ref,
                  wout_ref, n2g_ref, wr_ref, br_ref,
                  x1_ref, h2_ref, meta_ref, info_ref, cnt_ref,
                  sg_ref, sh_ref, carry_ref, proj_ref, o_ref):
    bi = pl.program_id(0)
    si = pl.program_id(1)

    @pl.when(si == 0)
    def _():
        sg_ref[...] = jnp.zeros_like(sg_ref)
        sh_ref[...] = jnp.zeros_like(sh_ref)

    @pl.when((bi == 0) & (si == 0))
    def _():
        carry_ref[...] = jnp.zeros_like(carry_ref)

    mod = mod_ref[0]
    sh1 = mod[0:1]
    gain1 = n1g_ref[...] * (1.0 + mod[1:2])
    for ti in range(TOK_TILE // MIX_SUB):
        rows = slice(ti * MIX_SUB, (ti + 1) * MIX_SUB)
        x = x_ref[0, rows, :]
        r1 = lax.rsqrt(jnp.mean(x * x, axis=-1, keepdims=True) + EPS)
        hb = (x * r1 * gain1 + sh1).astype(BF16)
        proj_ref[rows, :] = _dot(hb, win_ref[...])
        _mixer_rows(x_ref.at[0, rows, :], mod_ref, wa2_ref, ba_ref, glag_ref, hgg_ref,
                    lbl_ref, wout_ref, n2g_ref, x1_ref.at[0, rows, :], h2_ref.at[0, rows, :],
                    proj_ref.at[rows, :], o_ref.at[rows, :], sg_ref, sh_ref)
    _route(h2_ref.at[0], wr_ref, br_ref, meta_ref.at[0], info_ref.at[0], cnt_ref, carry_ref)


def _mixer_rows(x_ref, mod_ref, wa2_ref, ba_ref, glag_ref, hgg_ref, lbl_ref, wout_ref, n2g_ref,
                x1_ref, h2_ref, proj_ref, o_ref, sg_ref, sh_ref):
    tile = MIX_SUB
    mod = mod_ref[0]
    gt1, sh2, sc2 = mod[2:3], mod[3:4], mod[4:5]

    lbl = lbl_ref[...]
    lmax = jnp.max(lbl, axis=0, keepdims=True)
    lexp = jnp.exp(lbl - lmax)
    lb = lexp[0:1] / jnp.sum(lexp, axis=0, keepdims=True)

    row = lax.broadcasted_iota(jnp.int32, (CHUNK, CHUNK), 0)
    col = lax.broadcasted_iota(jnp.int32, (CHUNK, CHUNK), 1)
    causal = row >= col
    trow = lax.broadcasted_iota(jnp.int32, (tile, tile), 0)
    tcol = lax.broadcasted_iota(jnp.int32, (tile, tile), 1)
    same_chunk = (trow - tcol) <= (trow & (CHUNK - 1))
    blocktril = ((trow >= tcol) & same_chunk).astype(BF16)

    a_logit = _dot(proj_ref[:, C_GA:C_GA + LANES].astype(BF16), wa2_ref[...]) + ba_ref[...]
    log_sig = jnp.minimum(a_logit, 0.0) - jnp.log(1.0 + jnp.exp(-jnp.abs(a_logit)))
    q = proj_ref[:, C_GQ:C_GK] * (GLA_DK ** -0.5)
    o_a = _recurrence(q, proj_ref[:, C_GK:C_GV], proj_ref[:, C_GV:C_GG], log_sig * (1.0 / GLA_TAU),
                      sg_ref, blocktril, causal, GLA_HEADS, GLA_DK, GLA_DV)
    o_a = _head_norm(o_a, glag_ref[...], GLA_HEADS, GLA_DV) * _silu(proj_ref[:, C_GG:C_GA])
    o_ref[:, 0:GLA_W] = o_a.astype(BF16)
    hq = proj_ref[:, C_HQ:C_HF] * (HG_DK ** -0.5)
    sig = jax.nn.sigmoid(proj_ref[:, C_HF:C_HI])
    f = lb + (1.0 - lb) * sig
    kh = (1.0 - lb) * (1.0 - sig)
    o_b = _recurrence(hq, kh, proj_ref[:, C_HI:C_HG], jnp.log(f), sh_ref,
                      blocktril, causal, HG_HEADS, HG_DK, HG_DV)
    o_b = _head_norm(o_b, hgg_ref[...], HG_HEADS, HG_DV) * _silu(proj_ref[:, C_HG:C_END])
    o_ref[:, GLA_W:GLA_W + HG_W] = o_b.astype(BF16)

    mix = _dot(o_ref[...], wout_ref[...])
    x1 = x_ref[...] + gt1 * mix
    x1_ref[...] = x1
    r2 = lax.rsqrt(jnp.mean(x1 * x1, axis=-1, keepdims=True) + EPS)
    h2 = x1 * r2 * (n2g_ref[...] * (1.0 + sc2)) + sh2
    h2_ref[...] = h2.astype(BF16)


def _route(h2_ref, wr_ref, br_ref, meta_ref, info_ref, cnt_ref, carry_ref):
    tile = TOK_TILE
    logits = _dot(h2_ref[...], wr_ref[...]) + br_ref[...]
    lane = lax.broadcasted_iota(jnp.int32, (tile, LANES), 1)
    neg = jnp.float32(-jnp.inf)
    big = jnp.int32(1 << 20)
    gl = jnp.where((lane >= N_EXPERTS) & (lane < N_EXPERTS + N_GROUPS), logits, neg)
    gmax = jnp.max(gl, axis=-1, keepdims=True)
    gidx = jnp.min(jnp.where(gl == gmax, lane, big), axis=-1, keepdims=True) - N_EXPERTS
    g_p = 1.0 / jnp.sum(jnp.exp(gl - gmax), axis=-1, keepdims=True)
    in_group = (lane >= gidx * EXPERTS_PER_GROUP) & (lane < (gidx + 1) * EXPERTS_PER_GROUP)
    el = jnp.where(in_group, logits, neg)
    m1 = jnp.max(el, axis=-1, keepdims=True)
    i1 = jnp.min(jnp.where(el == m1, lane, big), axis=-1, keepdims=True)
    el2 = jnp.where(lane == i1, neg, el)
    m2 = jnp.max(el2, axis=-1, keepdims=True)
    i2 = jnp.min(jnp.where(el2 == m2, lane, big), axis=-1, keepdims=True)
    t = jnp.exp(m2 - m1)
    w1 = g_p / (1.0 + t)
    w2 = g_p * t / (1.0 + t)

    sel1 = lane == i1
    sel2 = lane == i2
    onehot = (sel1 | sel2).astype(BF16)
    trow = lax.broadcasted_iota(jnp.int32, (tile, tile), 0)
    tcol = lax.broadcasted_iota(jnp.int32, (tile, tile), 1)
    strict = (trow > tcol).astype(BF16)
    lcum = _dot(strict, onehot)
    cnt = jnp.sum(onehot.astype(F32), axis=0, keepdims=True)
    run = jnp.floor((cnt + (ROW_GRAIN - 1)) * (1.0 / ROW_GRAIN)) * ROW_GRAIN
    erow = lax.broadcasted_iota(jnp.int32, (LANES, LANES), 0)
    ecol = lax.broadcasted_iota(jnp.int32, (LANES, LANES), 1)
    before = (erow < ecol).astype(BF16)
    run8 = jnp.broadcast_to(run, (8, LANES))
    loff = _dot(run8.astype(BF16), before)
    pos = lcum + loff[0:1]
    p1 = jnp.sum(jnp.where(sel1, pos, 0.0), axis=-1, keepdims=True)
    p2 = jnp.sum(jnp.where(sel2, pos, 0.0), axis=-1, keepdims=True)

    srow = lax.broadcasted_iota(jnp.int32, (8, LANES), 0)
    info = jnp.where(srow == 0, run8, jnp.where(srow == 1, carry_ref[...], jnp.where(srow == 2, loff, 0.0)))
    info_ref[...] = info
    carry_ref[...] = carry_ref[...] + run
    cnt_ref[...] = carry_ref[...]

    meta = jnp.where(lane == 0, p1, 0.0)
    meta = jnp.where(lane == 1, p2, meta)
    meta = jnp.where(lane == 2, w1, meta)
    meta = jnp.where(lane == 3, w2, meta)
    meta_ref[...] = meta


def _mixer(x, modall, n1g, win, wa2, ba, glag, hgg, lbl, wout, n2g, wr, br):
    bsz, seq, dm = x.shape
    tile = TOK_TILE
    spt = seq // tile
    grid = (bsz, spt)
    const = lambda shape: pl.BlockSpec(shape, lambda b, s: (0,) * len(shape))
    tok = lambda width: pl.BlockSpec((1, tile, width), lambda b, s: (b, s, 0))
    return pl.pallas_call(
        _mixer_kernel,
        grid=grid,
        in_specs=[tok(dm),
                  pl.BlockSpec((1, 8, dm), lambda b, s: (b, 0, 0)),
                  const((1, dm)), const((dm, C_END)), const((LANES, GLA_QK)), const((1, GLA_QK)),
                  const((1, GLA_DV)), const((1, HG_DV)), const((2, HG_W)),
                  const((dm, dm)), const((1, dm)), const((dm, LANES)), const((1, LANES))],
        out_specs=[tok(dm), tok(dm), tok(LANES),
                   pl.BlockSpec((1, 8, LANES), lambda b, s: (b * spt + s, 0, 0)),
                   const((1, LANES))],
        out_shape=[jax.ShapeDtypeStruct((bsz, seq, dm), F32),
                   jax.ShapeDtypeStruct((bsz, seq, dm), BF16),
                   jax.ShapeDtypeStruct((bsz, seq, LANES), F32),
                   jax.ShapeDtypeStruct((bsz * spt, 8, LANES), F32),
                   jax.ShapeDtypeStruct((1, LANES), F32)],
        scratch_shapes=([pltpu.VMEM((GLA_HEADS, GLA_DV, GLA_DK), F32),
                         pltpu.VMEM((HG_HEADS, HG_DV, HG_DK), F32),
                         pltpu.VMEM((1, LANES), F32),
                         pltpu.VMEM((tile, C_END), F32),
                         pltpu.VMEM((tile, dm), BF16)]),
        compiler_params=pltpu.CompilerParams(
            dimension_semantics=("arbitrary", "arbitrary"), vmem_limit_bytes=VMEM_LIMIT),
        name="mixer",
    )(x, modall, n1g, win, wa2, ba, glag, hgg, lbl, wout, n2g, wr, br)


def _by_row_count(used_rows, extents, body):
    lo = 0
    for rows in extents:
        hi = rows
        pl.when((used_rows > lo) & (used_rows <= hi))(functools.partial(body, rows))
        lo = hi


def _local_positions(meta, rows):
    lane = lax.broadcasted_iota(jnp.int32, (meta.shape[0], rows), 1)
    p1 = meta[:, 0:1].astype(jnp.int32)
    p2 = meta[:, 1:2].astype(jnp.int32)
    return lane == p1, lane == p2


def _dispatch_kernel(pad_ref, np_ref, dst_ref, meta_ref, h2_ref, xs_ref, xl_ref, zero_ref, sem, zsem):
    @pl.when(pl.program_id(0) == 0)
    def _():
        zero_ref[...] = jnp.zeros_like(zero_ref)
        total = 0
        for e in range(N_EXPERTS):
            row = pad_ref[e]
            n_grain = pad_ref[N_EXPERTS + e] // ROW_GRAIN
            total = total + pad_ref[N_EXPERTS + e]
            for bit in reversed(range(ZERO_BITS)):
                size = ROW_GRAIN << bit
                on = (n_grain >> bit) & 1

                @pl.when(on == 1)
                def _():
                    pltpu.make_async_copy(
                        zero_ref.at[pl.ds(0, size), :],
                        xs_ref.at[pl.ds(pl.multiple_of(row, ROW_GRAIN), size), :], zsem).start()

                row = row + on * size

        @pl.when(total > 0)
        def _():
            rows = pl.ds(0, pl.multiple_of(total, ROW_GRAIN))
            pltpu.make_async_copy(xs_ref.at[rows, :], xs_ref.at[rows, :], zsem).wait()

    step = pl.program_id(0)
    slot = step % 2

    def sort_rows(rows):
        sel1, sel2 = _local_positions(meta_ref[...], rows)
        xl_ref[slot, 0:rows, :] = _dot_tn((sel1 | sel2).astype(BF16), h2_ref[...]).astype(BF16)

    _by_row_count(np_ref[step] * ROW_GRAIN, ROW_EXTENTS, sort_rows)

    def piece(j, buf, dst_row):
        src = xl_ref.at[buf, pl.ds(pl.multiple_of(j * ROW_GRAIN, ROW_GRAIN), ROW_GRAIN), :]
        dst = xs_ref.at[pl.ds(pl.multiple_of(dst_row, ROW_GRAIN), ROW_GRAIN), :]
        return pltpu.make_async_copy(src, dst, sem.at[buf])

    def issue(i, carry):
        for u in range(PIECE_UNROLL):
            j = i * PIECE_UNROLL + u
            piece(j, slot, dst_ref[j]).start()
        return carry

    lax.fori_loop(0, np_ref[step] // PIECE_UNROLL, issue, 0)

    def drain(buf, n):
        rows = pl.ds(0, n * ROW_GRAIN)
        pltpu.make_async_copy(xl_ref.at[buf, rows, :], xs_ref.at[rows, :], sem.at[buf]).wait()

    @pl.when(step > 0)
    def _():
        drain(1 - slot, np_ref[step - 1])

    @pl.when(step == pl.num_programs(0) - 1)
    def _():
        drain(slot, np_ref[step])


def _dispatch(pads, n_pieces, dst, meta, h2, n_rows):
    n, dm = h2.shape
    tile = TOK_TILE
    return pl.pallas_call(
        _dispatch_kernel,
        grid_spec=pltpu.PrefetchScalarGridSpec(
            num_scalar_prefetch=2,
            grid=(n // tile,),
            in_specs=[pl.BlockSpec((LANES,), lambda i, tl, npc: (i,), memory_space=pltpu.SMEM),
                      pl.BlockSpec((tile, LANES), lambda i, tl, npc: (i, 0)),
                      pl.BlockSpec((tile, dm), lambda i, tl, npc: (i, 0))],
            out_specs=pl.BlockSpec(memory_space=pl.ANY),
            scratch_shapes=[pltpu.VMEM((2, LOCAL_ROWS, dm), BF16),
                            pltpu.VMEM((ROW_GRAIN << (ZERO_BITS - 1), dm), BF16),
                            pltpu.SemaphoreType.DMA((2,)), pltpu.SemaphoreType.DMA(())]),
        out_shape=jax.ShapeDtypeStruct((n_rows, dm), BF16),
        compiler_params=pltpu.CompilerParams(
            dimension_semantics=("arbitrary",), vmem_limit_bytes=VMEM_LIMIT),
        name="dispatch",
    )(pads, n_pieces, dst, meta, h2)


def _experts_kernel(te_ref, nu_ref, valid_ref, xs_ref, wg_ref, wu_ref, wd_ref, ys_ref):
    tm = xs_ref.shape[0]

    def swiglu(rows):
        xb = xs_ref[0:rows, :]
        g = _dot(xb, wg_ref[0].astype(BF16))
        u = _dot(xb, wu_ref[0].astype(BF16))
        a = (_silu(g) * u).astype(BF16)
        ys_ref[0:rows, :] = _dot(a, wd_ref[0].astype(BF16)).astype(BF16)
        if rows < tm:
            ys_ref[rows:tm, :] = jnp.zeros((tm - rows, ys_ref.shape[1]), BF16)

    _by_row_count(valid_ref[pl.program_id(0)], EXPERT_EXTENTS, swiglu)


def _experts(tile_expert, n_used, valid, xs, wg, wu, wd, n_tiles):
    rows, dm = xs.shape
    tm = EXP_TILE
    ff = wg.shape[-1]

    def row_map(i, te, nu, vl):
        return (jnp.minimum(i, nu[0] - 1), 0)

    def out_map(i, te, nu, vl):
        return (jnp.where(i < nu[0], i, n_tiles), 0)

    def weight_map(i, te, nu, vl):
        return (te[i], 0, 0)

    return pl.pallas_call(
        _experts_kernel,
        grid_spec=pltpu.PrefetchScalarGridSpec(
            num_scalar_prefetch=3,
            grid=(n_tiles,),
            in_specs=[pl.BlockSpec((tm, dm), row_map),
                      pl.BlockSpec((1, dm, ff), weight_map),
                      pl.BlockSpec((1, dm, ff), weight_map),
                      pl.BlockSpec((1, ff, dm), weight_map)],
            out_specs=pl.BlockSpec((tm, dm), out_map)),
        out_shape=jax.ShapeDtypeStruct(((n_tiles + 1) * tm, dm), BF16),
        compiler_params=pltpu.CompilerParams(
            dimension_semantics=("arbitrary",), vmem_limit_bytes=VMEM_LIMIT),
        name="experts",
    )(tile_expert, n_used, valid, xs, wg, wu, wd)


def _combine_kernel(np_ref, src_ref, src_next_ref, x1_ref, meta_ref, mod_ref, nfg_ref, ys_ref, out_ref,
                    yl_ref, sem):
    step = pl.program_id(0) * pl.num_programs(1) + pl.program_id(1)
    last = pl.num_programs(0) * pl.num_programs(1) - 1
    slot = step % 2

    def piece(j, buf, src_row):
        src = ys_ref.at[pl.ds(pl.multiple_of(src_row, ROW_GRAIN), ROW_GRAIN), :]
        dst = yl_ref.at[buf, pl.ds(pl.multiple_of(j * ROW_GRAIN, ROW_GRAIN), ROW_GRAIN), :]
        return pltpu.make_async_copy(src, dst, sem.at[buf])

    def fetch(rows_ref, buf, count):
        def body(i, carry):
            for u in range(PIECE_UNROLL):
                j = i * PIECE_UNROLL + u
                piece(j, buf, rows_ref[j]).start()
            return carry

        lax.fori_loop(0, count // PIECE_UNROLL, body, 0)

    @pl.when(step == 0)
    def _():
        yl_ref[...] = jnp.zeros_like(yl_ref)
        fetch(src_ref, 0, np_ref[0])

    @pl.when(step < last)
    def _():
        fetch(src_next_ref, 1 - slot, np_ref[step + 1])

    rows = pl.ds(0, np_ref[step] * ROW_GRAIN)
    pltpu.make_async_copy(ys_ref.at[rows, :], yl_ref.at[slot, rows, :], sem.at[slot]).wait()

    mod = mod_ref[0]
    gt2, shf, scf = mod[5:6], mod[6:7], mod[7:8]
    meta = meta_ref[0]

    def combine_rows(rows):
        sel1, sel2 = _local_positions(meta, rows)
        wsel = jnp.where(sel1, meta[:, 2:3], jnp.where(sel2, meta[:, 3:4], 0.0))
        ffn = _dot(wsel.astype(BF16), yl_ref[slot, 0:rows, :])
        x2 = x1_ref[0] + gt2 * ffn
        r = lax.rsqrt(jnp.mean(x2 * x2, axis=-1, keepdims=True) + EPS)
        out_ref[0] = x2 * r * (nfg_ref[...] * (1.0 + scf)) + shf

    _by_row_count(np_ref[step] * ROW_GRAIN, ROW_EXTENTS, combine_rows)


def _combine(n_pieces, dst, x1, meta, modall, nfg, ys):
    bsz, seq, dm = x1.shape
    tile = TOK_TILE
    spt = seq // tile
    return pl.pallas_call(
        _combine_kernel,
        grid_spec=pltpu.PrefetchScalarGridSpec(
            num_scalar_prefetch=1,
            grid=(bsz, spt),
            in_specs=[pl.BlockSpec((LANES,), lambda b, s, npc: (b * spt + s,), memory_space=pltpu.SMEM),
                      pl.BlockSpec((LANES,), lambda b, s, npc: (jnp.minimum(b * spt + s + 1, bsz * spt - 1),),
                                   memory_space=pltpu.SMEM),
                      pl.BlockSpec((1, tile, dm), lambda b, s, npc: (b, s, 0)),
                      pl.BlockSpec((1, tile, LANES), lambda b, s, npc: (b, s, 0)),
                      pl.BlockSpec((1, 8, dm), lambda b, s, npc: (b, 0, 0)),
                      pl.BlockSpec((1, dm), lambda b, s, npc: (0, 0)),
                      pl.BlockSpec(memory_space=pl.ANY)],
            out_specs=pl.BlockSpec((1, tile, dm), lambda b, s, npc: (b, s, 0)),
            scratch_shapes=[pltpu.VMEM((2, LOCAL_ROWS, dm), BF16), pltpu.SemaphoreType.DMA((2,))]),
        out_shape=jax.ShapeDtypeStruct((bsz, seq, dm), F32),
        compiler_params=pltpu.CompilerParams(
            dimension_semantics=("arbitrary", "arbitrary"), vmem_limit_bytes=VMEM_LIMIT),
        name="combine",
    )(n_pieces, dst, dst, x1, meta, modall, nfg, ys)


def kernel(x, c, w_ada, b_ada, norm1_g, w_in, gla_w_a2, gla_b_a, gla_norm_g, hg_norm_g, hg_lb_logits,
           w_out, norm2_g, w_rg, b_rg, w_re, b_re, w_exp_gate, w_exp_up, w_exp_down, w_ada_final,
           b_ada_final, norm_f_g):
    bsz, seq, dm = x.shape
    n_tok = bsz * seq

    mod = _ada(c, w_ada[0], b_ada[0], 1536)
    mod_f = _ada(c, w_ada_final, b_ada_final, 1024)
    modall = jnp.concatenate([mod.reshape(bsz, 6, dm), mod_f.reshape(bsz, 2, dm)], axis=1)

    wi = w_in[0]
    ga0 = 2 * GLA_QK + 2 * GLA_W
    win = jnp.concatenate([wi[:, :ga0],
                           jnp.pad(wi[:, ga0:ga0 + GLA_RANK], ((0, 0), (0, LANES - GLA_RANK))),
                           wi[:, ga0 + GLA_RANK:]], axis=1).astype(BF16)
    wa2 = jnp.pad(gla_w_a2[0], ((0, LANES - GLA_RANK), (0, 0))).astype(BF16)
    pad_r = LANES - N_EXPERTS - N_GROUPS
    wr = jnp.pad(jnp.concatenate([w_re[0], w_rg[0]], axis=1), ((0, 0), (0, pad_r))).astype(BF16)
    br = jnp.pad(jnp.concatenate([b_re[0], b_rg[0]]), (0, pad_r)).reshape(1, LANES)

    assert seq % TOK_TILE == 0
    x1, h2, meta, info, total = _mixer(
        x, modall, norm1_g[0].reshape(1, dm), win, wa2, gla_b_a[0].reshape(1, GLA_QK),
        gla_norm_g[0].reshape(1, GLA_DV), hg_norm_g[0].reshape(1, HG_DV), hg_lb_logits,
        w_out[0].astype(BF16), norm2_g[0].reshape(1, dm), wr, br)

    tm = EXP_TILE
    n_tok_tiles = n_tok // TOK_TILE
    n_tiles = (2 * n_tok + n_tok_tiles * N_EXPERTS * (ROW_GRAIN - 1)) // tm + N_EXPERTS
    i32 = jnp.int32
    run = info[:, 0, :N_EXPERTS].astype(i32)
    before = info[:, 1, :N_EXPERTS].astype(i32)
    loff = info[:, 2, :N_EXPERTS].astype(i32)
    rows_e = total[0, :N_EXPERTS].astype(i32)
    region = ((rows_e + tm - 1) // tm) * tm
    ends = jnp.cumsum(region)
    pad_rows = region - rows_e
    pads = jnp.concatenate([ends - pad_rows, pad_rows]).astype(i32)
    gbase = (ends - region)[None, :] + before
    n_pieces = (jnp.sum(run, axis=1) // ROW_GRAIN).astype(i32)
    piece_row = jnp.arange(LANES, dtype=i32) * ROW_GRAIN
    owner = jnp.sum(piece_row[None, :, None] >= (loff + run)[:, None, :], axis=-1)
    owner = jnp.minimum(owner, N_EXPERTS - 1)
    shift = gbase - loff
    pick = owner[..., None] == jnp.arange(N_EXPERTS, dtype=i32)
    dst = (jnp.sum(jnp.where(pick, shift[:, None, :], 0), axis=-1) + piece_row[None, :])
    live = jnp.arange(LANES, dtype=i32)[None, :] < n_pieces[:, None]
    n_pieces = ((n_pieces + PIECE_UNROLL - 1) // PIECE_UNROLL) * PIECE_UNROLL
    lane_slot = jnp.arange(LANES, dtype=i32) % PIECE_UNROLL
    parity = (jnp.arange(n_tok_tiles, dtype=i32) % 2) * PIECE_UNROLL
    spare = n_tiles * tm + (parity[:, None] + lane_slot[None, :]) * ROW_GRAIN
    dst_out = jnp.where(live, dst, spare).astype(i32).reshape(n_tok_tiles * LANES)
    dst_in = jnp.where(live, dst, 0).astype(i32).reshape(n_tok_tiles * LANES)
    n_used = (ends[-1] // tm).astype(i32).reshape(1)
    tile_start = jnp.arange(n_tiles, dtype=i32) * tm
    tile_start = jnp.minimum(tile_start, ends[-1] - tm)
    tile_expert = jnp.sum(tile_start[:, None] >= ends[None, :], axis=1).astype(i32)
    rows_end = ends - pad_rows
    tile_end = jnp.sum(jnp.where(tile_expert[:, None] == jnp.arange(N_EXPERTS, dtype=i32), rows_end, 0), axis=1)
    valid = jnp.clip(tile_end - jnp.arange(n_tiles, dtype=i32) * tm, 0, tm).astype(i32)

    meta2 = meta.reshape(n_tok, LANES)
    spare_rows = 2 * PIECE_UNROLL * ROW_GRAIN
    xs = _dispatch(pads, n_pieces, dst_out, meta2, h2.reshape(n_tok, dm), n_tiles * tm + spare_rows)
    ys = _experts(tile_expert, n_used, valid, xs, w_exp_gate[0], w_exp_up[0], w_exp_down[0], n_tiles)
    return _combine(n_pieces, dst_in, x1, meta, modall, norm_f_g.reshape(1, dm), ys)
```
